```python
import math
import jax, jax.numpy as jnp
from jax import lax
import numpy as np

D_MODEL = 2048
BATCH = 8
SEQ = 8192
DEPTH = 1

D_RNN = D_MODEL
RG_HEADS = 16
RG_HEAD_DIM = D_RNN // RG_HEADS
CONV_WIDTH = 4
RG_C = 8.0
D_SSM = D_MODEL // 2
SSM_GROUP = 16
SSM_GROUPS = D_SSM // SSM_GROUP
SSM_STATE = 64
D_FF = 4 * D_MODEL
D_IN = 2 * D_RNN + D_SSM + 2 * D_MODEL
LN_EPS = 1e-5

kernel_name = "hybrid_rglru_s5_gated_deepnorm_block"


def _layernorm(x, g, b):
    xf = x.astype(jnp.float32)
    mu = jnp.mean(xf, axis=-1, keepdims=True)
    var = jnp.mean(jnp.square(xf - mu), axis=-1, keepdims=True)
    y = (xf - mu) * lax.rsqrt(var + LN_EPS)
    return (y * g.astype(jnp.float32) + b.astype(jnp.float32)).astype(x.dtype)


def _real_linear_scan(a, b):
    def combine(c1, c2):
        a1, b1 = c1
        a2, b2 = c2
        return a1 * a2, a2 * b1 + b2
    _, h = lax.associative_scan(combine, (a, b), axis=1)
    return h


def _complex_linear_scan(a_re, a_im, b_re, b_im):
    def combine(c1, c2):
        a1r, a1i, b1r, b1i = c1
        a2r, a2i, b2r, b2i = c2
        ar = a2r * a1r - a2i * a1i
        ai = a2r * a1i + a2i * a1r
        br = a2r * b1r - a2i * b1i + b2r
        bi = a2r * b1i + a2i * b1r + b2i
        return ar, ai, br, bi
    _, _, h_re, h_im = lax.associative_scan(combine, (a_re, a_im, b_re, b_im), axis=1)
    return h_re, h_im


def _causal_depthwise_conv(x, w, bias):
    c = x.shape[-1]
    y = lax.conv_general_dilated(
        x, w[:, None, :].astype(x.dtype), window_strides=(1,),
        padding=[(CONV_WIDTH - 1, 0)], dimension_numbers=("NWC", "WIO", "NWC"),
        feature_group_count=c)
    return y + bias


def _rglru_branch(xr, gate, conv_w, conv_b, wa, ba, wx, bx, lam, w_a_out):
    bsz, s, _ = xr.shape
    xc = _causal_depthwise_conv(xr, conv_w, conv_b)
    xh = xc.reshape(bsz, s, RG_HEADS, RG_HEAD_DIM)
    r = jax.nn.sigmoid(jnp.einsum("bshi,hij->bshj", xh, wa) + ba).reshape(bsz, s, D_RNN)
    i = jax.nn.sigmoid(jnp.einsum("bshi,hij->bshj", xh, wx) + bx).reshape(bsz, s, D_RNN)
    log_a = (-RG_C * r.astype(jnp.float32)) * jax.nn.softplus(-lam.astype(jnp.float32))
    a = jnp.exp(log_a)
    mult = jnp.sqrt(-jnp.expm1(2.0 * log_a))
    b = mult * (i.astype(jnp.float32) * xc.astype(jnp.float32))
    h = _real_linear_scan(a, b).astype(xr.dtype)
    return (h * jax.nn.gelu(gate)) @ w_a_out


def _s5_branch(u, a_re, a_im, log_dt, b_re, b_im, c_re, c_im, d, glu_w, glu_v):
    bsz, s, _ = u.shape
    uf = u.astype(jnp.float32).reshape(bsz, s, SSM_GROUPS, SSM_GROUP)
    dt = jnp.exp(log_dt.astype(jnp.float32))[:, None]
    lr = jnp.minimum(a_re.astype(jnp.float32), -1e-4)
    li = a_im.astype(jnp.float32)
    mag = jnp.exp(lr * dt)
    lbr = mag * jnp.cos(li * dt)
    lbi = mag * jnp.sin(li * dt)
    zr, zi = lbr - 1.0, lbi
    den = lr * lr + li * li
    fr = (zr * lr + zi * li) / den
    fi = (zi * lr - zr * li) / den
    br32, bi32 = b_re.astype(jnp.float32), b_im.astype(jnp.float32)
    bbr = fr[..., None] * br32 - fi[..., None] * bi32
    bbi = fr[..., None] * bi32 + fi[..., None] * br32
    bu_re = jnp.einsum("bsgh,gph->bsgp", uf, bbr)
    bu_im = jnp.einsum("bsgh,gph->bsgp", uf, bbi)
    shp = (1, s, SSM_GROUPS, SSM_STATE)
    h_re, h_im = _complex_linear_scan(jnp.broadcast_to(lbr, shp), jnp.broadcast_to(lbi, shp),
                                      bu_re, bu_im)
    y = (jnp.einsum("bsgp,ghp->bsgh", h_re, c_re.astype(jnp.float32))
         - jnp.einsum("bsgp,ghp->bsgh", h_im, c_im.astype(jnp.float32))
         + d.astype(jnp.float32) * uf)
    y = jax.nn.gelu(y.reshape(bsz, s, D_SSM)).astype(u.dtype)
    return (y @ glu_w) * jax.nn.sigmoid(y @ glu_v)


def _fwd_setup_inputs(seed: int = 0) -> dict:
    key = jax.random.key(seed)
    ks = jax.random.split(key, 32)
    L = DEPTH
    beta = (8.0 * DEPTH) ** -0.25

    def nrm(k, shape, scale):
        return jax.random.normal(k, shape, jnp.float32) * scale

    x = nrm(ks[0], (BATCH, SEQ, D_MODEL), 1.0)
    w_in = nrm(ks[1], (L, D_MODEL, D_IN), D_MODEL ** -0.5)
    conv_w = nrm(ks[2], (L, CONV_WIDTH, D_RNN), CONV_WIDTH ** -0.5)
    conv_b = nrm(ks[3], (L, D_RNN), 0.01)
    rg_wa = nrm(ks[4], (L, RG_HEADS, RG_HEAD_DIM, RG_HEAD_DIM), RG_HEAD_DIM ** -0.5)
    rg_ba = nrm(ks[5], (L, RG_HEADS, RG_HEAD_DIM), 0.01)
    rg_wx = nrm(ks[6], (L, RG_HEADS, RG_HEAD_DIM, RG_HEAD_DIM), RG_HEAD_DIM ** -0.5)
    rg_bx = nrm(ks[7], (L, RG_HEADS, RG_HEAD_DIM), 0.01)
    a_c = jax.random.uniform(ks[8], (L, D_RNN), jnp.float32, 0.9, 0.999)
    a0 = a_c ** (1.0 / RG_C)
    rg_lambda = jnp.log(a0) - jnp.log1p(-a0)
    w_a_out = nrm(ks[9], (L, D_RNN, D_MODEL), D_RNN ** -0.5)
    n = jnp.arange(SSM_STATE, dtype=jnp.float32)
    ssm_a_re = -0.5 + nrm(ks[10], (L, SSM_GROUPS, SSM_STATE), 0.01)
    ssm_a_im = math.pi * n + nrm(ks[11], (L, SSM_GROUPS, SSM_STATE), 0.01)
    ssm_log_dt = jax.random.uniform(ks[12], (L, SSM_GROUPS), jnp.float32,
                                    math.log(1e-3), math.log(1e-1))
    ssm_b_re = nrm(ks[13], (L, SSM_GROUPS, SSM_STATE, SSM_GROUP), (2.0 * SSM_GROUP) ** -0.5)
    ssm_b_im = nrm(ks[14], (L, SSM_GROUPS, SSM_STATE, SSM_GROUP), (2.0 * SSM_GROUP) ** -0.5)
    ssm_c_re = nrm(ks[15], (L, SSM_GROUPS, SSM_GROUP, SSM_STATE), (0.5 * SSM_STATE) ** -0.5)
    ssm_c_im = nrm(ks[16], (L, SSM_GROUPS, SSM_GROUP, SSM_STATE), (0.5 * SSM_STATE) ** -0.5)
    ssm_d = nrm(ks[17], (L, SSM_GROUPS, SSM_GROUP), 1.0)
    glu_w = nrm(ks[18], (L, D_SSM, D_MODEL), D_SSM ** -0.5)
    glu_v = nrm(ks[19], (L, D_SSM, D_MODEL), D_SSM ** -0.5)
    w_out = nrm(ks[20], (L, D_MODEL, D_MODEL), beta * D_MODEL ** -0.5)
    ln1_g = 1.0 + nrm(ks[21], (L, D_MODEL), 0.02)
    ln1_b = nrm(ks[22], (L, D_MODEL), 0.02)
    mlp_w_up = nrm(ks[23], (L, D_MODEL, D_FF), beta * D_MODEL ** -0.5)
    mlp_b_up = nrm(ks[24], (L, D_FF), 0.01)
    mlp_w_down = nrm(ks[25], (L, D_FF, D_MODEL), beta * D_FF ** -0.5)
    mlp_b_down = nrm(ks[26], (L, D_MODEL), 0.01)
    ln2_g = 1.0 + nrm(ks[27], (L, D_MODEL), 0.02)
    ln2_b = nrm(ks[28], (L, D_MODEL), 0.02)
    return {"x": x, "w_in": w_in, "conv_w": conv_w, "conv_b": conv_b,
            "rg_wa": rg_wa, "rg_ba": rg_ba, "rg_wx": rg_wx, "rg_bx": rg_bx,
            "rg_lambda": rg_lambda, "w_a_out": w_a_out,
            "ssm_a_re": ssm_a_re, "ssm_a_im": ssm_a_im, "ssm_log_dt": ssm_log_dt,
            "ssm_b_re": ssm_b_re, "ssm_b_im": ssm_b_im, "ssm_c_re": ssm_c_re,
            "ssm_c_im": ssm_c_im, "ssm_d": ssm_d, "glu_w": glu_w, "glu_v": glu_v,
            "w_out": w_out, "ln1_g": ln1_g, "ln1_b": ln1_b,
            "mlp_w_up": mlp_w_up, "mlp_b_up": mlp_b_up, "mlp_w_down": mlp_w_down,
            "mlp_b_down": mlp_b_down, "ln2_g": ln2_g, "ln2_b": ln2_b}


def _fwd_reference(x, w_in, conv_w, conv_b, rg_wa, rg_ba, rg_wx, rg_bx, rg_lambda, w_a_out,
              ssm_a_re, ssm_a_im, ssm_log_dt, ssm_b_re, ssm_b_im, ssm_c_re, ssm_c_im,
              ssm_d, glu_w, glu_v, w_out, ln1_g, ln1_b, mlp_w_up, mlp_b_up,
              mlp_w_down, mlp_b_down, ln2_g, ln2_b):
    alpha = (2.0 * DEPTH) ** 0.25
    splits = [D_RNN, 2 * D_RNN, 2 * D_RNN + D_SSM, 2 * D_RNN + D_SSM + D_MODEL]
    for l in range(DEPTH):
        z = x @ w_in[l]
        xr, gate_r, u_s, g_a, g_b = jnp.split(z, splits, axis=-1)
        y_a = _rglru_branch(xr, gate_r, conv_w[l], conv_b[l], rg_wa[l], rg_ba[l],
                            rg_wx[l], rg_bx[l], rg_lambda[l], w_a_out[l])
        y_b = _s5_branch(u_s, ssm_a_re[l], ssm_a_im[l], ssm_log_dt[l], ssm_b_re[l],
                         ssm_b_im[l], ssm_c_re[l], ssm_c_im[l], ssm_d[l], glu_w[l], glu_v[l])
        mix = jax.nn.sigmoid(g_a) * y_a + jax.nn.sigmoid(g_b) * y_b
        x = _layernorm(alpha * x + mix @ w_out[l], ln1_g[l], ln1_b[l])
        h = jnp.square(jax.nn.relu(x @ mlp_w_up[l] + mlp_b_up[l])) @ mlp_w_down[l] + mlp_b_down[l]
        x = _layernorm(alpha * x + h, ln2_g[l], ln2_b[l])
    return x


import jax as _jax
import jax.numpy as _jnp

TWIN_FORMAT = 'train_step'
FWD_PARAMS = ['x', 'w_in', 'conv_w', 'conv_b', 'rg_wa', 'rg_ba', 'rg_wx', 'rg_bx', 'rg_lambda', 'w_a_out', 'ssm_a_re', 'ssm_a_im', 'ssm_log_dt', 'ssm_b_re', 'ssm_b_im', 'ssm_c_re', 'ssm_c_im', 'ssm_d', 'glu_w', 'glu_v', 'w_out', 'ln1_g', 'ln1_b', 'mlp_w_up', 'mlp_b_up', 'mlp_w_down', 'mlp_b_down', 'ln2_g', 'ln2_b']
TWIN_WEIGHTS = ['w_in', 'conv_w', 'conv_b', 'rg_wa', 'rg_ba', 'rg_wx', 'rg_bx', 'rg_lambda', 'w_a_out', 'ssm_a_re', 'ssm_a_im', 'ssm_log_dt', 'ssm_b_re', 'ssm_b_im', 'ssm_c_re', 'ssm_c_im', 'ssm_d', 'glu_w', 'glu_v', 'w_out', 'ln1_g', 'ln1_b', 'mlp_w_up', 'mlp_b_up', 'mlp_w_down', 'mlp_b_down', 'ln2_g', 'ln2_b']
TWIN_DIFF_INPUT = 'x'
TWIN_INPUTS = ['x', 'w_in', 'conv_w', 'conv_b', 'rg_wa', 'rg_ba', 'rg_wx', 'rg_bx', 'rg_lambda', 'w_a_out', 'ssm_a_re', 'ssm_a_im', 'ssm_log_dt', 'ssm_b_re', 'ssm_b_im', 'ssm_c_re', 'ssm_c_im', 'ssm_d', 'glu_w', 'glu_v', 'w_out', 'ln1_g', 'ln1_b', 'mlp_w_up', 'mlp_b_up', 'mlp_w_down', 'mlp_b_down', 'ln2_g', 'ln2_b', 'loss_target', 'm_w_in', 'm_conv_w', 'm_conv_b', 'm_rg_wa', 'm_rg_ba', 'm_rg_wx', 'm_rg_bx', 'm_rg_lambda', 'm_w_a_out', 'm_ssm_a_re', 'm_ssm_a_im', 'm_ssm_log_dt', 'm_ssm_b_re', 'm_ssm_b_im', 'm_ssm_c_re', 'm_ssm_c_im', 'm_ssm_d', 'm_glu_w', 'm_glu_v', 'm_w_out', 'm_ln1_g', 'm_ln1_b', 'm_mlp_w_up', 'm_mlp_b_up', 'm_mlp_w_down', 'm_mlp_b_down', 'm_ln2_g', 'm_ln2_b', 'v_w_in', 'v_conv_w', 'v_conv_b', 'v_rg_wa', 'v_rg_ba', 'v_rg_wx', 'v_rg_bx', 'v_rg_lambda', 'v_w_a_out', 'v_ssm_a_re', 'v_ssm_a_im', 'v_ssm_log_dt', 'v_ssm_b_re', 'v_ssm_b_im', 'v_ssm_c_re', 'v_ssm_c_im', 'v_ssm_d', 'v_glu_w', 'v_glu_v', 'v_w_out', 'v_ln1_g', 'v_ln1_b', 'v_mlp_w_up', 'v_mlp_b_up', 'v_mlp_w_down', 'v_mlp_b_down', 'v_ln2_g', 'v_ln2_b']
TWIN_OUTPUTS = ['loss', 'grad_x', 'grad_w_in', 'grad_conv_w', 'grad_conv_b', 'grad_rg_wa', 'grad_rg_ba', 'grad_rg_wx', 'grad_rg_bx', 'grad_rg_lambda', 'grad_w_a_out', 'grad_ssm_a_re', 'grad_ssm_a_im', 'grad_ssm_log_dt', 'grad_ssm_b_re', 'grad_ssm_b_im', 'grad_ssm_c_re', 'grad_ssm_c_im', 'grad_ssm_d', 'grad_glu_w', 'grad_glu_v', 'grad_w_out', 'grad_ln1_g', 'grad_ln1_b', 'grad_mlp_w_up', 'grad_mlp_b_up', 'grad_mlp_w_down', 'grad_mlp_b_down', 'grad_ln2_g', 'grad_ln2_b', 'delta_w_in', 'delta_conv_w', 'delta_conv_b', 'delta_rg_wa', 'delta_rg_ba', 'delta_rg_wx', 'delta_rg_bx', 'delta_rg_lambda', 'delta_w_a_out', 'delta_ssm_a_re', 'delta_ssm_a_im', 'delta_ssm_log_dt', 'delta_ssm_b_re', 'delta_ssm_b_im', 'delta_ssm_c_re', 'delta_ssm_c_im', 'delta_ssm_d', 'delta_glu_w', 'delta_glu_v', 'delta_w_out', 'delta_ln1_g', 'delta_ln1_b', 'delta_mlp_w_up', 'delta_mlp_b_up', 'delta_mlp_w_down', 'delta_mlp_b_down', 'delta_ln2_g', 'delta_ln2_b', 'new_m_w_in', 'new_m_conv_w', 'new_m_conv_b', 'new_m_rg_wa', 'new_m_rg_ba', 'new_m_rg_wx', 'new_m_rg_bx', 'new_m_rg_lambda', 'new_m_w_a_out', 'new_m_ssm_a_re', 'new_m_ssm_a_im', 'new_m_ssm_log_dt', 'new_m_ssm_b_re', 'new_m_ssm_b_im', 'new_m_ssm_c_re', 'new_m_ssm_c_im', 'new_m_ssm_d', 'new_m_glu_w', 'new_m_glu_v', 'new_m_w_out', 'new_m_ln1_g', 'new_m_ln1_b', 'new_m_mlp_w_up', 'new_m_mlp_b_up', 'new_m_mlp_w_down', 'new_m_mlp_b_down', 'new_m_ln2_g', 'new_m_ln2_b', 'new_v_w_in', 'new_v_conv_w', 'new_v_conv_b', 'new_v_rg_wa', 'new_v_rg_ba', 'new_v_rg_wx', 'new_v_rg_bx', 'new_v_rg_lambda', 'new_v_w_a_out', 'new_v_ssm_a_re', 'new_v_ssm_a_im', 'new_v_ssm_log_dt', 'new_v_ssm_b_re', 'new_v_ssm_b_im', 'new_v_ssm_c_re', 'new_v_ssm_c_im', 'new_v_ssm_d', 'new_v_glu_w', 'new_v_glu_v', 'new_v_w_out', 'new_v_ln1_g', 'new_v_ln1_b', 'new_v_mlp_w_up', 'new_v_mlp_b_up', 'new_v_mlp_w_down', 'new_v_mlp_b_down', 'new_v_ln2_g', 'new_v_ln2_b']
TWIN_LEAF_KINDS = {'loss': 'loss', 'grad_x': 'grad_x', 'grad_w_in': 'grad_w', 'grad_conv_w': 'grad_w', 'grad_conv_b': 'grad_w', 'grad_rg_wa': 'grad_w', 'grad_rg_ba': 'grad_w', 'grad_rg_wx': 'grad_w', 'grad_rg_bx': 'grad_w', 'grad_rg_lambda': 'grad_w', 'grad_w_a_out': 'grad_w', 'grad_ssm_a_re': 'grad_w', 'grad_ssm_a_im': 'grad_w', 'grad_ssm_log_dt': 'grad_w', 'grad_ssm_b_re': 'grad_w', 'grad_ssm_b_im': 'grad_w', 'grad_ssm_c_re': 'grad_w', 'grad_ssm_c_im': 'grad_w', 'grad_ssm_d': 'grad_w', 'grad_glu_w': 'grad_w', 'grad_glu_v': 'grad_w', 'grad_w_out': 'grad_w', 'grad_ln1_g': 'grad_w', 'grad_ln1_b': 'grad_w', 'grad_mlp_w_up': 'grad_w', 'grad_mlp_b_up': 'grad_w', 'grad_mlp_w_down': 'grad_w', 'grad_mlp_b_down': 'grad_w', 'grad_ln2_g': 'grad_w', 'grad_ln2_b': 'grad_w', 'delta_w_in': 'delta_w', 'delta_conv_w': 'delta_w', 'delta_conv_b': 'delta_w', 'delta_rg_wa': 'delta_w', 'delta_rg_ba': 'delta_w', 'delta_rg_wx': 'delta_w', 'delta_rg_bx': 'delta_w', 'delta_rg_lambda': 'delta_w', 'delta_w_a_out': 'delta_w', 'delta_ssm_a_re': 'delta_w', 'delta_ssm_a_im': 'delta_w', 'delta_ssm_log_dt': 'delta_w', 'delta_ssm_b_re': 'delta_w', 'delta_ssm_b_im': 'delta_w', 'delta_ssm_c_re': 'delta_w', 'delta_ssm_c_im': 'delta_w', 'delta_ssm_d': 'delta_w', 'delta_glu_w': 'delta_w', 'delta_glu_v': 'delta_w', 'delta_w_out': 'delta_w', 'delta_ln1_g': 'delta_w', 'delta_ln1_b': 'delta_w', 'delta_mlp_w_up': 'delta_w', 'delta_mlp_b_up': 'delta_w', 'delta_mlp_w_down': 'delta_w', 'delta_mlp_b_down': 'delta_w', 'delta_ln2_g': 'delta_w', 'delta_ln2_b': 'delta_w', 'new_m_w_in': 'new_m', 'new_m_conv_w': 'new_m', 'new_m_conv_b': 'new_m', 'new_m_rg_wa': 'new_m', 'new_m_rg_ba': 'new_m', 'new_m_rg_wx': 'new_m', 'new_m_rg_bx': 'new_m', 'new_m_rg_lambda': 'new_m', 'new_m_w_a_out': 'new_m', 'new_m_ssm_a_re': 'new_m', 'new_m_ssm_a_im': 'new_m', 'new_m_ssm_log_dt': 'new_m', 'new_m_ssm_b_re': 'new_m', 'new_m_ssm_b_im': 'new_m', 'new_m_ssm_c_re': 'new_m', 'new_m_ssm_c_im': 'new_m', 'new_m_ssm_d': 'new_m', 'new_m_glu_w': 'new_m', 'new_m_glu_v': 'new_m', 'new_m_w_out': 'new_m', 'new_m_ln1_g': 'new_m', 'new_m_ln1_b': 'new_m', 'new_m_mlp_w_up': 'new_m', 'new_m_mlp_b_up': 'new_m', 'new_m_mlp_w_down': 'new_m', 'new_m_mlp_b_down': 'new_m', 'new_m_ln2_g': 'new_m', 'new_m_ln2_b': 'new_m', 'new_v_w_in': 'new_v', 'new_v_conv_w': 'new_v', 'new_v_conv_b': 'new_v', 'new_v_rg_wa': 'new_v', 'new_v_rg_ba': 'new_v', 'new_v_rg_wx': 'new_v', 'new_v_rg_bx': 'new_v', 'new_v_rg_lambda': 'new_v', 'new_v_w_a_out': 'new_v', 'new_v_ssm_a_re': 'new_v', 'new_v_ssm_a_im': 'new_v', 'new_v_ssm_log_dt': 'new_v', 'new_v_ssm_b_re': 'new_v', 'new_v_ssm_b_im': 'new_v', 'new_v_ssm_c_re': 'new_v', 'new_v_ssm_c_im': 'new_v', 'new_v_ssm_d': 'new_v', 'new_v_glu_w': 'new_v', 'new_v_glu_v': 'new_v', 'new_v_w_out': 'new_v', 'new_v_ln1_g': 'new_v', 'new_v_ln1_b': 'new_v', 'new_v_mlp_w_up': 'new_v', 'new_v_mlp_b_up': 'new_v', 'new_v_mlp_w_down': 'new_v', 'new_v_mlp_b_down': 'new_v', 'new_v_ln2_g': 'new_v', 'new_v_ln2_b': 'new_v'}


def _forward(args):
    return _fwd_reference(*[args[k] for k in FWD_PARAMS])


def _output_shape():
    def fwd():
        inp = _fwd_setup_inputs(0)
        return _fwd_reference(*[inp[k] for k in FWD_PARAMS])
    out = _jax.eval_shape(fwd)
    return out.shape, out.dtype

N_MICROBATCH = 1
ADAM_LR = 0.001
ADAM_B1 = 0.9
ADAM_B2 = 0.999
ADAM_EPS = 1e-08
ADAM_WD = 0.01
ADAM_STEP = 10
PER_EXAMPLE_BATCH_AXIS = {'x': 0, 'loss_target': 0}
SHARED_INPUTS = []
_WEIGHT_DTYPES = {'w_in': _jnp.float32, 'conv_w': _jnp.float32, 'conv_b': _jnp.float32, 'rg_wa': _jnp.float32, 'rg_ba': _jnp.float32, 'rg_wx': _jnp.float32, 'rg_bx': _jnp.float32, 'rg_lambda': _jnp.float32, 'w_a_out': _jnp.float32, 'ssm_a_re': _jnp.float32, 'ssm_a_im': _jnp.float32, 'ssm_log_dt': _jnp.float32, 'ssm_b_re': _jnp.float32, 'ssm_b_im': _jnp.float32, 'ssm_c_re': _jnp.float32, 'ssm_c_im': _jnp.float32, 'ssm_d': _jnp.float32, 'glu_w': _jnp.float32, 'glu_v': _jnp.float32, 'w_out': _jnp.float32, 'ln1_g': _jnp.float32, 'ln1_b': _jnp.float32, 'mlp_w_up': _jnp.float32, 'mlp_b_up': _jnp.float32, 'mlp_w_down': _jnp.float32, 'mlp_b_down': _jnp.float32, 'ln2_g': _jnp.float32, 'ln2_b': _jnp.float32}
MOMENT_SCALE = {'w_in': 1.198276e-02, 'conv_w': 1.707215e-02, 'conv_b': 3.029996e-01, 'rg_wa': 7.310626e-03, 'rg_ba': 6.128297e-03, 'rg_wx': 1.346164e-02, 'rg_bx': 5.668075e-03, 'rg_lambda': 1.045035e-02, 'w_a_out': 1.731355e-02, 'ssm_a_re': 1.887949e-03, 'ssm_a_im': 1.999643e-03, 'ssm_log_dt': 1.995689e+00, 'ssm_b_re': 1.219934e-03, 'ssm_b_im': 1.219600e-03, 'ssm_c_re': 1.192217e-03, 'ssm_c_im': 1.213627e-03, 'ssm_d': 3.106724e-02, 'glu_w': 1.982440e-02, 'glu_v': 3.933607e-03, 'w_out': 4.310565e-02, 'ln1_g': 1.076054e+00, 'ln1_b': 5.329871e-01, 'mlp_w_up': 2.584003e-02, 'mlp_b_up': 5.646051e-02, 'mlp_w_down': 7.320386e-02, 'mlp_b_down': 4.486489e-01, 'ln2_g': 3.198594e+01, 'ln2_b': 3.125604e+00}


def _to_microbatches(a, axis):
    t = _jnp.moveaxis(a, axis, 0)
    t = t.reshape((N_MICROBATCH, t.shape[0] // N_MICROBATCH) + t.shape[1:])
    return _jnp.moveaxis(t, 1, axis + 1)


def setup_inputs(seed: int = 0) -> dict:
    inp = _fwd_setup_inputs(seed)
    key = _jax.random.fold_in(_jax.random.key(seed), 7919)
    shape, _ = _output_shape()
    out = dict(inp)
    out["loss_target"] = _jax.random.normal(_jax.random.fold_in(key, 0), shape, _jnp.float32)
    for i, name in enumerate(TWIN_WEIGHTS):
        w = inp[name].astype(_jnp.float32)
        if MOMENT_SCALE is None:
            s = _jnp.sqrt(_jnp.mean(_jnp.square(w)) + 1e-30)
        else:
            s = MOMENT_SCALE[name]
        km, kv = _jax.random.split(_jax.random.fold_in(key, i + 1))
        out[name] = w
        out["m_" + name] = s * _jax.random.normal(km, w.shape, _jnp.float32)
        out["v_" + name] = (s * s) * _jax.random.uniform(kv, w.shape, _jnp.float32, 0.5, 1.5)
    if N_MICROBATCH > 1:
        for name, axis in PER_EXAMPLE_BATCH_AXIS.items():
            out[name] = _to_microbatches(out[name], axis)
    return {'x': out['x'], 'w_in': out['w_in'], 'conv_w': out['conv_w'], 'conv_b': out['conv_b'], 'rg_wa': out['rg_wa'], 'rg_ba': out['rg_ba'], 'rg_wx': out['rg_wx'], 'rg_bx': out['rg_bx'], 'rg_lambda': out['rg_lambda'], 'w_a_out': out['w_a_out'], 'ssm_a_re': out['ssm_a_re'], 'ssm_a_im': out['ssm_a_im'], 'ssm_log_dt': out['ssm_log_dt'], 'ssm_b_re': out['ssm_b_re'], 'ssm_b_im': out['ssm_b_im'], 'ssm_c_re': out['ssm_c_re'], 'ssm_c_im': out['ssm_c_im'], 'ssm_d': out['ssm_d'], 'glu_w': out['glu_w'], 'glu_v': out['glu_v'], 'w_out': out['w_out'], 'ln1_g': out['ln1_g'], 'ln1_b': out['ln1_b'], 'mlp_w_up': out['mlp_w_up'], 'mlp_b_up': out['mlp_b_up'], 'mlp_w_down': out['mlp_w_down'], 'mlp_b_down': out['mlp_b_down'], 'ln2_g': out['ln2_g'], 'ln2_b': out['ln2_b'], 'loss_target': out['loss_target'], 'm_w_in': out['m_w_in'], 'm_conv_w': out['m_conv_w'], 'm_conv_b': out['m_conv_b'], 'm_rg_wa': out['m_rg_wa'], 'm_rg_ba': out['m_rg_ba'], 'm_rg_wx': out['m_rg_wx'], 'm_rg_bx': out['m_rg_bx'], 'm_rg_lambda': out['m_rg_lambda'], 'm_w_a_out': out['m_w_a_out'], 'm_ssm_a_re': out['m_ssm_a_re'], 'm_ssm_a_im': out['m_ssm_a_im'], 'm_ssm_log_dt': out['m_ssm_log_dt'], 'm_ssm_b_re': out['m_ssm_b_re'], 'm_ssm_b_im': out['m_ssm_b_im'], 'm_ssm_c_re': out['m_ssm_c_re'], 'm_ssm_c_im': out['m_ssm_c_im'], 'm_ssm_d': out['m_ssm_d'], 'm_glu_w': out['m_glu_w'], 'm_glu_v': out['m_glu_v'], 'm_w_out': out['m_w_out'], 'm_ln1_g': out['m_ln1_g'], 'm_ln1_b': out['m_ln1_b'], 'm_mlp_w_up': out['m_mlp_w_up'], 'm_mlp_b_up': out['m_mlp_b_up'], 'm_mlp_w_down': out['m_mlp_w_down'], 'm_mlp_b_down': out['m_mlp_b_down'], 'm_ln2_g': out['m_ln2_g'], 'm_ln2_b': out['m_ln2_b'], 'v_w_in': out['v_w_in'], 'v_conv_w': out['v_conv_w'], 'v_conv_b': out['v_conv_b'], 'v_rg_wa': out['v_rg_wa'], 'v_rg_ba': out['v_rg_ba'], 'v_rg_wx': out['v_rg_wx'], 'v_rg_bx': out['v_rg_bx'], 'v_rg_lambda': out['v_rg_lambda'], 'v_w_a_out': out['v_w_a_out'], 'v_ssm_a_re': out['v_ssm_a_re'], 'v_ssm_a_im': out['v_ssm_a_im'], 'v_ssm_log_dt': out['v_ssm_log_dt'], 'v_ssm_b_re': out['v_ssm_b_re'], 'v_ssm_b_im': out['v_ssm_b_im'], 'v_ssm_c_re': out['v_ssm_c_re'], 'v_ssm_c_im': out['v_ssm_c_im'], 'v_ssm_d': out['v_ssm_d'], 'v_glu_w': out['v_glu_w'], 'v_glu_v': out['v_glu_v'], 'v_w_out': out['v_w_out'], 'v_ln1_g': out['v_ln1_g'], 'v_ln1_b': out['v_ln1_b'], 'v_mlp_w_up': out['v_mlp_w_up'], 'v_mlp_b_up': out['v_mlp_b_up'], 'v_mlp_w_down': out['v_mlp_w_down'], 'v_mlp_b_down': out['v_mlp_b_down'], 'v_ln2_g': out['v_ln2_g'], 'v_ln2_b': out['v_ln2_b']}


def _loss(weights, diff, rest, loss_target):
    with _jax.named_scope("forward"):
        args = {**rest, TWIN_DIFF_INPUT: diff, **{k: w.astype(_WEIGHT_DTYPES[k]) for k, w in weights.items()}}
        y = _forward(args)
    with _jax.named_scope("loss_head"):
        err = _jnp.square(y.astype(_jnp.float32) - loss_target)
        return 0.5 * _jnp.sum(_jnp.mean(err, axis=-1)) if err.ndim else 0.5 * err


def _adamw(w, g, m, v):
    m = ADAM_B1 * m + (1.0 - ADAM_B1) * g
    v = ADAM_B2 * v + (1.0 - ADAM_B2) * _jnp.square(g)
    m_hat = m / (1.0 - ADAM_B1 ** ADAM_STEP)
    v_hat = v / (1.0 - ADAM_B2 ** ADAM_STEP)
    delta = -ADAM_LR * (m_hat / (_jnp.sqrt(v_hat) + ADAM_EPS) + ADAM_WD * w)
    return delta, m, v


def reference(x, w_in, conv_w, conv_b, rg_wa, rg_ba, rg_wx, rg_bx, rg_lambda, w_a_out, ssm_a_re, ssm_a_im, ssm_log_dt, ssm_b_re, ssm_b_im, ssm_c_re, ssm_c_im, ssm_d, glu_w, glu_v, w_out, ln1_g, ln1_b, mlp_w_up, mlp_b_up, mlp_w_down, mlp_b_down, ln2_g, ln2_b, loss_target, m_w_in, m_conv_w, m_conv_b, m_rg_wa, m_rg_ba, m_rg_wx, m_rg_bx, m_rg_lambda, m_w_a_out, m_ssm_a_re, m_ssm_a_im, m_ssm_log_dt, m_ssm_b_re, m_ssm_b_im, m_ssm_c_re, m_ssm_c_im, m_ssm_d, m_glu_w, m_glu_v, m_w_out, m_ln1_g, m_ln1_b, m_mlp_w_up, m_mlp_b_up, m_mlp_w_down, m_mlp_b_down, m_ln2_g, m_ln2_b, v_w_in, v_conv_w, v_conv_b, v_rg_wa, v_rg_ba, v_rg_wx, v_rg_bx, v_rg_lambda, v_w_a_out, v_ssm_a_re, v_ssm_a_im, v_ssm_log_dt, v_ssm_b_re, v_ssm_b_im, v_ssm_c_re, v_ssm_c_im, v_ssm_d, v_glu_w, v_glu_v, v_w_out, v_ln1_g, v_ln1_b, v_mlp_w_up, v_mlp_b_up, v_mlp_w_down, v_mlp_b_down, v_ln2_g, v_ln2_b):
    given = dict(x=x, w_in=w_in, conv_w=conv_w, conv_b=conv_b, rg_wa=rg_wa, rg_ba=rg_ba, rg_wx=rg_wx, rg_bx=rg_bx, rg_lambda=rg_lambda, w_a_out=w_a_out, ssm_a_re=ssm_a_re, ssm_a_im=ssm_a_im, ssm_log_dt=ssm_log_dt, ssm_b_re=ssm_b_re, ssm_b_im=ssm_b_im, ssm_c_re=ssm_c_re, ssm_c_im=ssm_c_im, ssm_d=ssm_d, glu_w=glu_w, glu_v=glu_v, w_out=w_out, ln1_g=ln1_g, ln1_b=ln1_b, mlp_w_up=mlp_w_up, mlp_b_up=mlp_b_up, mlp_w_down=mlp_w_down, mlp_b_down=mlp_b_down, ln2_g=ln2_g, ln2_b=ln2_b, loss_target=loss_target, m_w_in=m_w_in, m_conv_w=m_conv_w, m_conv_b=m_conv_b, m_rg_wa=m_rg_wa, m_rg_ba=m_rg_ba, m_rg_wx=m_rg_wx, m_rg_bx=m_rg_bx, m_rg_lambda=m_rg_lambda, m_w_a_out=m_w_a_out, m_ssm_a_re=m_ssm_a_re, m_ssm_a_im=m_ssm_a_im, m_ssm_log_dt=m_ssm_log_dt, m_ssm_b_re=m_ssm_b_re, m_ssm_b_im=m_ssm_b_im, m_ssm_c_re=m_ssm_c_re, m_ssm_c_im=m_ssm_c_im, m_ssm_d=m_ssm_d, m_glu_w=m_glu_w, m_glu_v=m_glu_v, m_w_out=m_w_out, m_ln1_g=m_ln1_g, m_ln1_b=m_ln1_b, m_mlp_w_up=m_mlp_w_up, m_mlp_b_up=m_mlp_b_up, m_mlp_w_down=m_mlp_w_down, m_mlp_b_down=m_mlp_b_down, m_ln2_g=m_ln2_g, m_ln2_b=m_ln2_b, v_w_in=v_w_in, v_conv_w=v_conv_w, v_conv_b=v_conv_b, v_rg_wa=v_rg_wa, v_rg_ba=v_rg_ba, v_rg_wx=v_rg_wx, v_rg_bx=v_rg_bx, v_rg_lambda=v_rg_lambda, v_w_a_out=v_w_a_out, v_ssm_a_re=v_ssm_a_re, v_ssm_a_im=v_ssm_a_im, v_ssm_log_dt=v_ssm_log_dt, v_ssm_b_re=v_ssm_b_re, v_ssm_b_im=v_ssm_b_im, v_ssm_c_re=v_ssm_c_re, v_ssm_c_im=v_ssm_c_im, v_ssm_d=v_ssm_d, v_glu_w=v_glu_w, v_glu_v=v_glu_v, v_w_out=v_w_out, v_ln1_g=v_ln1_g, v_ln1_b=v_ln1_b, v_mlp_w_up=v_mlp_w_up, v_mlp_b_up=v_mlp_b_up, v_mlp_w_down=v_mlp_w_down, v_mlp_b_down=v_mlp_b_down, v_ln2_g=v_ln2_g, v_ln2_b=v_ln2_b)
    weights = {n: given[n] for n in TWIN_WEIGHTS}
    shared = {n: given[n] for n in SHARED_INPUTS}
    per_example = {n: given[n] for n in ['x']}
    grad_fn = _jax.value_and_grad(_loss, argnums=(0, 1))

    def one_microbatch(ex, loss_target):
        ex = dict(ex)
        diff = ex.pop(TWIN_DIFF_INPUT)
        return grad_fn(weights, diff, {**shared, **ex}, loss_target)

    if N_MICROBATCH == 1:
        loss, (grad_w, grad_x) = one_microbatch(per_example, given["loss_target"])
    else:
        def body(carry, xs):
            loss_sum, grad_sum = carry
            l_k, (gw_k, gx_k) = one_microbatch(xs[0], xs[1])
            with _jax.named_scope("update"):
                return (loss_sum + l_k, _jax.tree.map(_jnp.add, grad_sum, gw_k)), gx_k

        init = (_jnp.zeros((), _jnp.float32), _jax.tree.map(_jnp.zeros_like, weights))
        (loss, grad_w), grad_x = _jax.lax.scan(body, init, (per_example, given["loss_target"]))
    with _jax.named_scope("update"):
        delta_w, new_m, new_v = {}, {}, {}
        for n in TWIN_WEIGHTS:
            delta_w[n], new_m[n], new_v[n] = _adamw(weights[n], grad_w[n], given["m_" + n], given["v_" + n])
    return (loss, grad_x, *[grad_w[n] for n in TWIN_WEIGHTS], *[delta_w[n] for n in TWIN_WEIGHTS],
            *[new_m[n] for n in TWIN_WEIGHTS], *[new_v[n] for n in TWIN_WEIGHTS])
```

```python
import functools
import math

import jax
import jax.numpy as jnp
from jax import lax
from jax.experimental import pallas as pl
from jax.experimental.pallas import tpu as pltpu

F32 = jnp.float32
BF16 = jnp.bfloat16
MESH = pl.DeviceIdType.MESH
ANY = pl.BlockSpec(memory_space=pl.ANY)

ALPHA = 2.0 ** 0.25
LN_EPS = 1e-5
RG_C = 8.0
ADAM_LR, ADAM_B1, ADAM_B2, ADAM_EPS, ADAM_WD, ADAM_STEP = 0.001, 0.9, 0.999, 1e-08, 0.01, 10
GELU_K0 = math.sqrt(2.0 / math.pi)
GELU_K1 = 0.044715
VMEM_LIMIT = 56 * 1024 * 1024
N_CHIPS = 4
SCAN_ROWS = 8


def _pcall(body, **kw):
    return pl.pallas_call(body, **kw)


def _cparams(n_grid):
    return pltpu.CompilerParams(dimension_semantics=("arbitrary",) * n_grid, vmem_limit_bytes=VMEM_LIMIT)


def _gelu(x):
    return 0.5 * x * (1.0 + jnp.tanh(GELU_K0 * (x + GELU_K1 * x * x * x)))


def _gelu_grad(x):
    t = jnp.tanh(GELU_K0 * (x + GELU_K1 * x * x * x))
    return 0.5 * (1.0 + t) + 0.5 * x * (1.0 - t * t) * GELU_K0 * (1.0 + 3.0 * GELU_K1 * x * x)


def _neg_expm1(x):
    series = x * (1.0 + x * (0.5 + x * (1.0 / 6.0 + x * (1.0 / 24.0))))
    return -jnp.where(x > -0.03, series, jnp.exp(x) - 1.0)


def _rows_part(v):
    m, n = v.shape
    return v.reshape(m // 8, 8, n).sum(axis=0)


def _dot(a, b):
    return jnp.dot(a, b, preferred_element_type=F32)


def _dot_nt(a, b):
    return lax.dot_general(a, b, (((1,), (1,)), ((), ())), preferred_element_type=F32)


def _dot_tn(a, b):
    return lax.dot_general(a, b, (((0,), (0,)), ((), ())), preferred_element_type=F32)


NN = (((1,), (0,)), ((), ()))
NT = (((1,), (1,)), ((), ()))
TN = (((0,), (0,)), ((), ()))


def _mm(name, a, b, a_spec, b_spec, contract, grid, out_shape, out_specs, epilogue,
        extras=(), extra_specs=(), acc_shape=None, aliases=None):
    nk = grid[2]
    ne = len(extras)
    n_out = len(out_shape)

    def body(*refs):
        a_ref, b_ref = refs[0], refs[1]
        ex = refs[2:2 + ne]
        outs = refs[2 + ne:2 + ne + n_out]
        prod = lax.dot_general(a_ref[...].astype(BF16), b_ref[...].astype(BF16), contract,
                               preferred_element_type=F32)
        if nk == 1:
            epilogue(prod, ex, outs)
        else:
            acc = refs[2 + ne + n_out]
            k = pl.program_id(2)

            @pl.when(k == 0)
            def _():
                acc[...] = prod

            @pl.when(k > 0)
            def _():
                acc[...] += prod

            @pl.when(k == nk - 1)
            def _():
                epilogue(acc[...], ex, outs)

    scratch = [pltpu.VMEM(acc_shape, F32)] if nk > 1 else []
    return _pcall(body, grid=grid, in_specs=[a_spec, b_spec, *extra_specs], out_specs=list(out_specs),
                  out_shape=list(out_shape), scratch_shapes=scratch, input_output_aliases=aliases or {},
                  compiler_params=_cparams(3), name=name)(a, b, *extras)


def _store(dtype=None):
    def epi(acc, ex, outs):
        outs[0][...] = acc.astype(outs[0].dtype)
    return epi


def _sds(shape, dtype):
    return jax.ShapeDtypeStruct(tuple(shape), dtype)


def _row_tile(rows, cols, budget=1 << 20):
    t = rows
    while t * cols * 4 > budget and t % 16 == 0:
        t //= 2
    return t


def _cast_bf16(name, w):
    r, c = w.shape
    tr = _row_tile(r, c)

    def body(w_ref, o_ref):
        o_ref[...] = w_ref[...].astype(BF16)

    return _pcall(body, grid=(r // tr,), in_specs=[pl.BlockSpec((tr, c), lambda i: (i, 0))],
                  out_specs=pl.BlockSpec((tr, c), lambda i: (i, 0)), out_shape=_sds((r, c), BF16),
                  compiler_params=_cparams(1), name=name)(w)


def _adamw(name, w, g, m, v):
    r, c = w.shape
    tr = _row_tile(r, c)

    def body(w_ref, g_ref, m_ref, v_ref, d_ref, nm_ref, nv_ref):
        gg = g_ref[...]
        nm = ADAM_B1 * m_ref[...] + (1.0 - ADAM_B1) * gg
        nv = ADAM_B2 * v_ref[...] + (1.0 - ADAM_B2) * (gg * gg)
        m_hat = nm / (1.0 - ADAM_B1 ** ADAM_STEP)
        v_hat = nv / (1.0 - ADAM_B2 ** ADAM_STEP)
        d_ref[...] = -ADAM_LR * (m_hat / (jnp.sqrt(v_hat) + ADAM_EPS) + ADAM_WD * w_ref[...])
        nm_ref[...] = nm
        nv_ref[...] = nv

    spec = pl.BlockSpec((tr, c), lambda i: (i, 0))
    return _pcall(body, grid=(r // tr,), in_specs=[spec] * 4, out_specs=[spec] * 3,
                  out_shape=[_sds((r, c), F32)] * 3, compiler_params=_cparams(1), name=name)(w, g, m, v)


def _add_half(name, g, recv, c_arr):
    _, _, rh, c = g.shape
    tr = _row_tile(rh, c)

    def body(c_ref, g_ref, r_ref, o_ref):
        o_ref[...] = g_ref[...] + r_ref[...]

    spec3 = pl.BlockSpec((None, tr, c), lambda s, r, cc: (s, r, 0))
    gs = pltpu.PrefetchScalarGridSpec(
        num_scalar_prefetch=1, grid=(N_CHIPS, rh // tr),
        in_specs=[pl.BlockSpec((None, None, tr, c), lambda s, r, cc: (s, cc[0], r, 0)), spec3],
        out_specs=spec3)
    return _pcall(body, grid_spec=gs, out_shape=_sds((N_CHIPS, rh, c), F32),
                  compiler_params=_cparams(2), name=name)(c_arr, g, recv)


def _add_chips(name, h, recv, chip_arr):
    _, rh, c = h.shape
    tr = _row_tile(rh, c)

    def body(k_ref, h_ref, r0, r1, r2, o_ref):
        o_ref[...] = ((h_ref[...] + r0[...]) + r1[...]) + r2[...]

    def rspec(k):
        return pl.BlockSpec((None, tr, c), lambda r, kk, k=k: (k, r, 0))

    gs = pltpu.PrefetchScalarGridSpec(
        num_scalar_prefetch=1, grid=(rh // tr,),
        in_specs=[pl.BlockSpec((None, tr, c), lambda r, kk: (kk[0], r, 0)), rspec(0), rspec(1), rspec(2)],
        out_specs=pl.BlockSpec((tr, c), lambda r, kk: (r, 0)))
    return _pcall(body, grid_spec=gs, out_shape=_sds((rh, c), F32),
                  compiler_params=_cparams(1), name=name)(chip_arr, h, recv, recv, recv)


def _me():
    return lax.axis_index("x"), lax.axis_index("y"), lax.axis_index("c")


def _peer_chip(x, y, k):
    px = (x + (k >> 1)) % 2
    py = (y + (k & 1)) % 2
    return px, py, 2 * px + py


def _all_gather_chips(name, w2):
    _, rh, c_ = w2.shape

    def body(in_ref, out_ref, send_sems, recv_sems, loc_sem):
        x, y, c = _me()
        chip = 2 * x + y
        sib = (x, y, 1 - c)
        loc = pltpu.make_async_copy(in_ref, out_ref.at[chip], loc_sem)
        loc.start()
        started = []
        for k in (1, 2, 3):
            px, py, _ = _peer_chip(x, y, k)
            cp = pltpu.make_async_remote_copy(src_ref=in_ref.at[c], dst_ref=out_ref.at[chip, c],
                                              send_sem=send_sems.at[k - 1], recv_sem=recv_sems.at[k - 1],
                                              device_id=(px, py, c), device_id_type=MESH)
            cp.start()
            started.append(cp)
        for k in (1, 2, 3):
            _, _, pchip = _peer_chip(x, y, k)
            pltpu.make_async_remote_copy(src_ref=in_ref.at[c], dst_ref=out_ref.at[pchip, c],
                                         send_sem=send_sems.at[k - 1], recv_sem=recv_sems.at[k - 1],
                                         device_id=sib, device_id_type=MESH).wait_recv()
            fw = pltpu.make_async_remote_copy(src_ref=out_ref.at[pchip, c], dst_ref=out_ref.at[pchip, c],
                                              send_sem=send_sems.at[2 + k], recv_sem=recv_sems.at[2 + k],
                                              device_id=sib, device_id_type=MESH)
            fw.start()
            started.append(fw)
        for k in (1, 2, 3):
            _, _, pchip = _peer_chip(x, y, k)
            pltpu.make_async_remote_copy(src_ref=in_ref.at[c], dst_ref=out_ref.at[pchip, 1 - c],
                                         send_sem=send_sems.at[2 + k], recv_sem=recv_sems.at[2 + k],
                                         device_id=sib, device_id_type=MESH).wait_recv()
        for cp in started:
            cp.wait_send()
        loc.wait()

    return _pcall(body, in_specs=[ANY], out_specs=ANY, out_shape=_sds((N_CHIPS, 2, rh, c_), w2.dtype),
                  scratch_shapes=[pltpu.SemaphoreType.DMA((6,)), pltpu.SemaphoreType.DMA((6,)),
                                  pltpu.SemaphoreType.DMA], name=name)(w2)


def _rs_to_sibling(name, gs):
    n = len(gs)

    def body(*refs):
        ins, outs = refs[:n], refs[n:2 * n]
        send_sems, recv_sems = refs[2 * n], refs[2 * n + 1]
        x, y, c = _me()
        sib = (x, y, 1 - c)
        cps = []
        for a in range(n):
            cp = pltpu.make_async_remote_copy(src_ref=ins[a].at[:, 1 - c], dst_ref=outs[a],
                                              send_sem=send_sems.at[a], recv_sem=recv_sems.at[a],
                                              device_id=sib, device_id_type=MESH)
            cp.start()
            cps.append(cp)
        for cp in cps:
            cp.wait_recv()
        for cp in cps:
            cp.wait_send()

    shapes = [_sds((g.shape[0], g.shape[2], g.shape[3]), F32) for g in gs]
    return _pcall(body, in_specs=[ANY] * n, out_specs=[ANY] * n, out_shape=shapes,
                  scratch_shapes=[pltpu.SemaphoreType.DMA((n,)), pltpu.SemaphoreType.DMA((n,))], name=name)(*gs)


def _rs_to_chips(name, hs):
    n = len(hs)

    def body(*refs):
        ins, outs = refs[:n], refs[n:2 * n]
        send_sems, recv_sems = refs[2 * n], refs[2 * n + 1]
        x, y, c = _me()
        cps = []
        for a in range(n):
            for k in (1, 2, 3):
                px, py, pchip = _peer_chip(x, y, k)
                cp = pltpu.make_async_remote_copy(src_ref=ins[a].at[pchip], dst_ref=outs[a].at[k - 1],
                                                  send_sem=send_sems.at[a, k - 1], recv_sem=recv_sems.at[a, k - 1],
                                                  device_id=(px, py, c), device_id_type=MESH)
                cp.start()
                cps.append(cp)
        for cp in cps:
            cp.wait_recv()
        for cp in cps:
            cp.wait_send()

    shapes = [_sds((3, h.shape[1], h.shape[2]), F32) for h in hs]
    return _pcall(body, in_specs=[ANY] * n, out_specs=[ANY] * n, out_shape=shapes,
                  scratch_shapes=[pltpu.SemaphoreType.DMA((n, 3)), pltpu.SemaphoreType.DMA((n, 3))], name=name)(*hs)


def _rs_join(name, rs):
    n = len(rs)

    def body(*refs):
        ins, outs = refs[:n], refs[n:2 * n]
        send_sems, recv_sems, loc_sems = refs[2 * n], refs[2 * n + 1], refs[2 * n + 2]
        x, y, c = _me()
        sib = (x, y, 1 - c)
        cps, locs = [], []
        for a in range(n):
            lc = pltpu.make_async_copy(ins[a], outs[a].at[c], loc_sems.at[a])
            lc.start()
            locs.append(lc)
            cp = pltpu.make_async_remote_copy(src_ref=ins[a], dst_ref=outs[a].at[c],
                                              send_sem=send_sems.at[a], recv_sem=recv_sems.at[a],
                                              device_id=sib, device_id_type=MESH)
            cp.start()
            cps.append(cp)
        for a in range(n):
            pltpu.make_async_remote_copy(src_ref=ins[a], dst_ref=outs[a].at[1 - c],
                                         send_sem=send_sems.at[a], recv_sem=recv_sems.at[a],
                                         device_id=sib, device_id_type=MESH).wait_recv()
        for cp in cps:
            cp.wait_send()
        for lc in locs:
            lc.wait()

    shapes = [_sds((2, r.shape[0], r.shape[1]), F32) for r in rs]
    return _pcall(body, in_specs=[ANY] * n, out_specs=[ANY] * n, out_shape=shapes,
                  scratch_shapes=[pltpu.SemaphoreType.DMA((n,)), pltpu.SemaphoreType.DMA((n,)),
                                  pltpu.SemaphoreType.DMA((n,))], name=name)(*rs)


def _all_reduce_small(name, p):
    _, r, w = p.shape

    def body(in_ref, out_ref, recv, send_sems, recv_sems):
        x, y, c = _me()
        me = 4 * x + 2 * y + c

        def peer(k):
            px, py, pc = (x + (k >> 2)) % 2, (y + ((k >> 1) & 1)) % 2, (c + (k & 1)) % 2
            return (px, py, pc), 4 * px + 2 * py + pc

        cps = []
        for k in range(1, 8):
            dev, idx = peer(k)
            cp = pltpu.make_async_remote_copy(src_ref=in_ref.at[idx], dst_ref=recv.at[k],
                                              send_sem=send_sems.at[0, k], recv_sem=recv_sems.at[0, k],
                                              device_id=dev, device_id_type=MESH)
            cp.start()
            cps.append(cp)
        for cp in cps:
            cp.wait_recv()
        acc = in_ref[me]
        for k in range(1, 8):
            acc = acc + recv[k]
        out_ref[me] = acc
        cps2 = []
        for k in range(1, 8):
            dev, idx = peer(k)
            cp = pltpu.make_async_remote_copy(src_ref=out_ref.at[me], dst_ref=out_ref.at[me],
                                              send_sem=send_sems.at[1, k], recv_sem=recv_sems.at[1, k],
                                              device_id=dev, device_id_type=MESH)
            cp.start()
            cps2.append(cp)
        for k in range(1, 8):
            dev, idx = peer(k)
            pltpu.make_async_remote_copy(src_ref=out_ref.at[me], dst_ref=out_ref.at[idx],
                                         send_sem=send_sems.at[1, k], recv_sem=recv_sems.at[1, k],
                                         device_id=dev, device_id_type=MESH).wait_recv()
        for cp in cps + cps2:
            cp.wait_send()

    vm = pl.BlockSpec(memory_space=pltpu.VMEM)
    return _pcall(body, in_specs=[vm], out_specs=vm, out_shape=_sds(p.shape, F32),
                  scratch_shapes=[pltpu.VMEM((8, r, w), F32), pltpu.SemaphoreType.DMA((2, 8)),
                                  pltpu.SemaphoreType.DMA((2, 8))],
                  compiler_params=pltpu.CompilerParams(vmem_limit_bytes=VMEM_LIMIT), name=name)(p)


def _rglru_gates(xc, h, wa_ref, ba_ref, wx_ref, bx_ref, la_ref, hs):
    xb = xc.astype(BF16)
    r = jax.nn.sigmoid(_dot(xb, wa_ref[h]) + ba_ref[:, hs])
    ig = jax.nn.sigmoid(_dot(xb, wx_ref[h]) + bx_ref[:, hs])
    log_a = (-RG_C * r) * la_ref[:, hs]
    a = jnp.exp(log_a)
    mult = jnp.sqrt(_neg_expm1(2.0 * log_a))
    return r, ig, a, mult


def _conv_taps(cw_ref, cb_ref, xext, t, hs):
    acc = cb_ref[:, hs] + cw_ref[0:1, hs] * xext[pl.ds(5, t), hs]
    for k in range(1, 4):
        acc = acc + cw_ref[k:k + 1, hs] * xext[pl.ds(5 + k, t), hs]
    return acc


def _rglru_fwd(z, conv_w, conv_b, wa_b, ba, wx_b, bx, la, s, dr, t):
    nh = wa_b.shape[0]
    hd = dr // nh
    lw = dr // SCAN_ROWS
    hpr = lw // hd

    def body(xr_ref, gt_ref, cw_ref, cb_ref, wa_ref, ba_ref, wx_ref, bx_ref, la_ref, h_ref, hg_ref,
             xext, a_s, b_s, hcar):
        i = pl.program_id(0)

        @pl.when(i == 0)
        def _():
            xext[0:8, :] = jnp.zeros((8, dr), F32)
            hcar[...] = jnp.zeros((hpr, 8, hd), F32)

        @pl.when(i > 0)
        def _():
            xext[0:8, :] = xext[t:t + 8, :]

        xext[8:t + 8, :] = xr_ref[...]
        for h in range(nh):
            hs = slice(h * hd, (h + 1) * hd)
            xc = _conv_taps(cw_ref, cb_ref, xext, t, hs)
            _, ig, a, mult = _rglru_gates(xc, h, wa_ref, ba_ref, wx_ref, bx_ref, la_ref, hs)
            a_s[h % hpr, pl.ds(h // hpr, t, stride=8), :] = a
            b_s[h % hpr, pl.ds(h // hpr, t, stride=8), :] = mult * (ig * xc)

        def step(tt, hp):
            o = pl.multiple_of(tt * 8, 8)
            hn = a_s[:, pl.ds(o, 8), :] * hp + b_s[:, pl.ds(o, 8), :]
            b_s[:, pl.ds(o, 8), :] = hn
            return hn

        hcar[...] = lax.fori_loop(0, t, step, hcar[...], unroll=8)
        for h in range(nh):
            hs = slice(h * hd, (h + 1) * hd)
            hj = b_s[h % hpr, pl.ds(h // hpr, t, stride=8), :]
            h_ref[:, hs] = hj
            hg_ref[:, hs] = (hj * _gelu(gt_ref[:, hs])).astype(BF16)

    full = lambda arr: pl.BlockSpec(arr.shape, lambda i: (0,) * arr.ndim)
    return _pcall(
        body, grid=(s // t,),
        in_specs=[pl.BlockSpec((t, dr), lambda i: (i, 0)), pl.BlockSpec((t, dr), lambda i: (i, 1)),
                  full(conv_w), full(conv_b), full(wa_b), full(ba), full(wx_b), full(bx), full(la)],
        out_specs=[pl.BlockSpec((t, dr), lambda i: (i, 0))] * 2,
        out_shape=[_sds((s, dr), F32), _sds((s, dr), BF16)],
        scratch_shapes=[pltpu.VMEM((t + 8, dr), F32), pltpu.VMEM((hpr, t * 8, hd), F32),
                        pltpu.VMEM((hpr, t * 8, hd), F32), pltpu.VMEM((hpr, 8, hd), F32)],
        compiler_params=_cparams(1), name="rglru_fwd")(z, z, conv_w, conv_b, wa_b, ba, wx_b, bx, la)


def _rglru_bwd(dhg, z, hsave, conv_w, conv_b, wa_b, ba, wx_b, bx, la, s, dr, din, t):
    nh = wa_b.shape[0]
    hd = dr // nh
    lw = dr // SCAN_ROWS
    hpr = lw // hd
    nch = s // t
    tb = t // 8

    def body(dhg_ref, xr_ref, xp_ref, gt_ref, h_ref, hp_ref, cw_ref, cb_ref, wa_ref, ba_ref, wx_ref, bx_ref, la_ref,
             dz_ref, dcw_ref, dcb_ref, dwa_ref, dwx_ref, dba_ref, dbx_ref, dla_ref,
             xext, hext, r_s, i_s, a_s, g_s, dxe, car):
        i = pl.program_id(0)
        first = (nch - 1 - i) == 0

        @pl.when(i == 0)
        def _():
            for ref in (dcw_ref, dcb_ref, dwa_ref, dwx_ref, dba_ref, dbx_ref, dla_ref, car):
                ref[...] = jnp.zeros(ref.shape, F32)
            dxe[t:t + 8, :] = jnp.zeros((8, dr), F32)

        keep = jnp.where(first, 0.0, 1.0)
        xext[0:8, :] = xp_ref[...] * keep
        hext[0:8, :] = hp_ref[...] * keep
        xext[8:t + 8, :] = xr_ref[...]
        hext[8:t + 8, :] = h_ref[...]
        for h in range(nh):
            hs = slice(h * hd, (h + 1) * hd)
            xc = _conv_taps(cw_ref, cb_ref, xext, t, hs)
            r, ig, a, _ = _rglru_gates(xc, h, wa_ref, ba_ref, wx_ref, bx_ref, la_ref, hs)
            r_s[:, hs] = r
            i_s[:, hs] = ig
            a_s[h % hpr, pl.ds(h // hpr, t, stride=8), :] = a
            gate = gt_ref[:, hs]
            dh_out = dhg_ref[:, hs]
            g_s[h % hpr, pl.ds(h // hpr, t, stride=8), :] = dh_out * _gelu(gate)
            dz_ref[:, dr + h * hd:dr + (h + 1) * hd] = (dh_out * h_ref[:, hs] * _gelu_grad(gate)).astype(BF16)

        def step(k, cr):
            o = pl.multiple_of((t - 1 - k) * 8, 8)
            dh = g_s[:, pl.ds(o, 8), :] + cr
            g_s[:, pl.ds(o, 8), :] = dh
            return a_s[:, pl.ds(o, 8), :] * dh

        car[...] = lax.fori_loop(0, t, step, car[...], unroll=8)

        for h in range(nh):
            hs = slice(h * hd, (h + 1) * hd)
            dh = g_s[h % hpr, pl.ds(h // hpr, t, stride=8), :]
            a = a_s[h % hpr, pl.ds(h // hpr, t, stride=8), :]
            r = r_s[:, hs]
            ig = i_s[:, hs]
            xc = _conv_taps(cw_ref, cb_ref, xext, t, hs)
            la_h = la_ref[:, hs]
            mult = jnp.sqrt(_neg_expm1(2.0 * ((-RG_C * r) * la_h)))
            da = dh * hext[pl.ds(7, t), hs]
            dmult = dh * (ig * xc)
            dlog_a = da * a - dmult * (a * a) / mult
            dr_ = dlog_a * (-RG_C * la_h)
            dla_ref[:, hs] += _rows_part(dlog_a * (-RG_C * r))
            dpr = dr_ * r * (1.0 - r)
            dpi = (dh * mult * xc) * ig * (1.0 - ig)
            dprb, dpib, xcb = dpr.astype(BF16), dpi.astype(BF16), xc.astype(BF16)
            dxc = dh * mult * ig + _dot_nt(dprb, wa_ref[h]) + _dot_nt(dpib, wx_ref[h])
            dwa_ref[h] += _dot_tn(xcb, dprb)
            dwx_ref[h] += _dot_tn(xcb, dpib)
            dba_ref[:, hs] += _rows_part(dpr)
            dbx_ref[:, hs] += _rows_part(dpi)
            dcb_ref[:, hs] += _rows_part(dxc)
            dxe[0:t, hs] = dxc
            for k in range(4):
                dcw_ref[8 * k:8 * k + 8, hs] += _rows_part(dxc * xext[pl.ds(5 + k, t), hs])
        for h in range(nh):
            hs = slice(h * hd, (h + 1) * hd)
            dxr = cw_ref[3:4, hs] * dxe[pl.ds(0, t), hs]
            for k in range(3):
                dxr = dxr + cw_ref[k:k + 1, hs] * dxe[pl.ds(3 - k, t), hs]
            dz_ref[:, hs] = dxr.astype(BF16)
        dxe[t:t + 8, :] = dxe[0:8, :]

    full = lambda arr: pl.BlockSpec(arr.shape, lambda i: (0,) * arr.ndim)
    rev = lambda col: (lambda i: (nch - 1 - i, col))
    prev = lambda i: (jnp.maximum((nch - 1 - i) * tb - 1, 0), 0)
    acc = lambda shape: pl.BlockSpec(shape, lambda i: (0,) * len(shape))
    return _pcall(
        body, grid=(nch,),
        in_specs=[pl.BlockSpec((t, dr), rev(0)), pl.BlockSpec((t, dr), rev(0)), pl.BlockSpec((8, dr), prev),
                  pl.BlockSpec((t, dr), rev(1)), pl.BlockSpec((t, dr), rev(0)), pl.BlockSpec((8, dr), prev),
                  full(conv_w), full(conv_b), full(wa_b), full(ba), full(wx_b), full(bx), full(la)],
        out_specs=[pl.BlockSpec((t, 2 * dr), rev(0)), acc((32, dr)), acc((8, dr)), acc((nh, hd, hd)), acc((nh, hd, hd)),
                   acc((8, dr)), acc((8, dr)), acc((8, dr))],
        out_shape=[_sds((s, din), BF16), _sds((32, dr), F32), _sds((8, dr), F32), _sds((nh, hd, hd), F32),
                   _sds((nh, hd, hd), F32), _sds((8, dr), F32), _sds((8, dr), F32), _sds((8, dr), F32)],
        scratch_shapes=[pltpu.VMEM((t + 8, dr), F32), pltpu.VMEM((t + 8, dr), F32), pltpu.VMEM((t, dr), F32),
                        pltpu.VMEM((t, dr), F32), pltpu.VMEM((hpr, t * 8, hd), F32), pltpu.VMEM((hpr, t * 8, hd), F32),
                        pltpu.VMEM((t + 8, dr), F32), pltpu.VMEM((hpr, 8, hd), F32)],
        compiler_params=_cparams(1), name="rglru_bwd")(dhg, z, z, z, hsave, hsave, conv_w, conv_b, wa_b, ba, wx_b, bx, la)


LANES = 128


def _scan_put(ref, j, t, val):
    for q in range(ref.shape[0]):
        ref[q, pl.ds(j, t, stride=8), :] = val[:, q * LANES:(q + 1) * LANES]


def _scan_get(ref, j, t):
    return jnp.concatenate([ref[q, pl.ds(j, t, stride=8), :] for q in range(ref.shape[0])], axis=1)


def _s5_fwd(z, bre, bim, cre, cim, lbr, lbi, dvec, s, ds_, t, ucol):
    uw = ds_ // SCAN_ROWS
    nq = lbr.shape[0]
    tile = (nq, 8, LANES)

    def body(u_ref, bre_ref, bim_ref, cre_ref, cim_ref, lbr_ref, lbi_ref, d_ref, ys_ref, yg_ref, hre_ref, him_ref,
             car_re, car_im):
        i = pl.program_id(0)

        @pl.when(i == 0)
        def _():
            car_re[...] = jnp.zeros(tile, F32)
            car_im[...] = jnp.zeros(tile, F32)

        for j in range(SCAN_ROWS):
            uj = u_ref[:, j * uw:(j + 1) * uw].astype(BF16)
            _scan_put(hre_ref, j, t, _dot(uj, bre_ref[j]))
            _scan_put(him_ref, j, t, _dot(uj, bim_ref[j]))
        lr = lbr_ref[...]
        li = lbi_ref[...]

        def step(tt, cr):
            hr, hi = cr
            o = pl.multiple_of(tt * 8, 8)
            nr = lr * hr - li * hi + hre_ref[:, pl.ds(o, 8), :]
            ni = lr * hi + li * hr + him_ref[:, pl.ds(o, 8), :]
            hre_ref[:, pl.ds(o, 8), :] = nr
            him_ref[:, pl.ds(o, 8), :] = ni
            return nr, ni

        cr, ci = lax.fori_loop(0, t, step, (car_re[...], car_im[...]), unroll=8)
        car_re[...] = cr
        car_im[...] = ci
        for j in range(SCAN_ROWS):
            js = slice(j * uw, (j + 1) * uw)
            hr = _scan_get(hre_ref, j, t).astype(BF16)
            hi = _scan_get(him_ref, j, t).astype(BF16)
            yv = _dot(hr, cre_ref[j]) - _dot(hi, cim_ref[j]) + d_ref[:, js] * u_ref[:, js]
            ys_ref[:, js] = yv
            yg_ref[:, js] = _gelu(yv).astype(BF16)

    full = lambda arr: pl.BlockSpec(arr.shape, lambda i: (0,) * arr.ndim)
    hblk = pl.BlockSpec((nq, t * 8, LANES), lambda i: (0, i, 0))
    return _pcall(
        body, grid=(s // t,),
        in_specs=[pl.BlockSpec((t, ds_), lambda i: (i, ucol)), full(bre), full(bim), full(cre), full(cim),
                  full(lbr), full(lbi), full(dvec)],
        out_specs=[pl.BlockSpec((t, ds_), lambda i: (i, 0)), pl.BlockSpec((t, ds_), lambda i: (i, 0)), hblk, hblk],
        out_shape=[_sds((s, ds_), F32), _sds((s, ds_), BF16), _sds((nq, s * 8, LANES), F32),
                   _sds((nq, s * 8, LANES), F32)],
        scratch_shapes=[pltpu.VMEM(tile, F32), pltpu.VMEM(tile, F32)],
        compiler_params=_cparams(1), name="s5_fwd")(z, bre, bim, cre, cim, lbr, lbi, dvec)


def _s5_bwd(dys, z, hre, him, bre, bim, cre, cim, lbr, lbi, dvec, dz, s, ds_, t, ucol):
    uw = ds_ // SCAN_ROWS
    nq = lbr.shape[0]
    tile = (nq, 8, LANES)
    nch = s // t

    def body(dy_ref, u_ref, hre_ref, him_ref, hpr_ref, hpi_ref, bre_ref, bim_ref, cre_ref, cim_ref, lbr_ref, lbi_ref,
             d_ref, dzin_ref, dz_ref, dbre_ref, dbim_ref, dcre_ref, dcim_ref, dlr_ref, dli_ref, dd_ref,
             gre, gim, car_re, car_im):
        i = pl.program_id(0)
        first = (nch - 1 - i) == 0

        @pl.when(i == 0)
        def _():
            for ref in (dbre_ref, dbim_ref, dcre_ref, dcim_ref, dlr_ref, dli_ref, dd_ref, car_re, car_im):
                ref[...] = jnp.zeros(ref.shape, F32)

        for j in range(SCAN_ROWS):
            dyj = dy_ref[:, j * uw:(j + 1) * uw].astype(BF16)
            _scan_put(gre, j, t, _dot_nt(dyj, cre_ref[j]))
            _scan_put(gim, j, t, -_dot_nt(dyj, cim_ref[j]))
        lr = lbr_ref[...]
        li = lbi_ref[...]

        def one(o, hm_re, hm_im, cr):
            c_re, c_im, a_lr, a_li = cr
            g_re = gre[:, pl.ds(o, 8), :] + c_re
            g_im = gim[:, pl.ds(o, 8), :] + c_im
            gre[:, pl.ds(o, 8), :] = g_re
            gim[:, pl.ds(o, 8), :] = g_im
            a_lr = a_lr + (g_re * hm_re + g_im * hm_im)
            a_li = a_li + (g_im * hm_re - g_re * hm_im)
            return lr * g_re + li * g_im, lr * g_im - li * g_re, a_lr, a_li

        def step(k, cr):
            o = pl.multiple_of((t - 1 - k) * 8, 8)
            om = pl.multiple_of((t - 2 - k) * 8, 8)
            return one(o, hre_ref[:, pl.ds(om, 8), :], him_ref[:, pl.ds(om, 8), :], cr)

        zero = jnp.zeros(tile, F32)
        cr = lax.fori_loop(0, t - 1, step, (car_re[...], car_im[...], zero, zero), unroll=8)
        keep = jnp.where(first, 0.0, 1.0)
        c_re, c_im, a_lr, a_li = one(0, hpr_ref[...] * keep, hpi_ref[...] * keep, cr)
        car_re[...] = c_re
        car_im[...] = c_im
        dlr_ref[...] += a_lr
        dli_ref[...] += a_li
        for j in range(SCAN_ROWS):
            js = slice(j * uw, (j + 1) * uw)
            g_r = _scan_get(gre, j, t).astype(BF16)
            g_i = _scan_get(gim, j, t).astype(BF16)
            dyj = dy_ref[:, js]
            uj = u_ref[:, js]
            du = _dot_nt(g_r, bre_ref[j]) + _dot_nt(g_i, bim_ref[j]) + d_ref[:, js] * dyj
            dz_ref[:, js] = du.astype(BF16)
            ujb, dyjb = uj.astype(BF16), dyj.astype(BF16)
            dbre_ref[j] += _dot_tn(ujb, g_r)
            dbim_ref[j] += _dot_tn(ujb, g_i)
            h_r = _scan_get(hre_ref, j, t).astype(BF16)
            h_i = _scan_get(him_ref, j, t).astype(BF16)
            dcre_ref[j] += _dot_tn(h_r, dyjb)
            dcim_ref[j] -= _dot_tn(h_i, dyjb)
            dd_ref[:, js] += _rows_part(dyj * uj)

    full = lambda arr: pl.BlockSpec(arr.shape, lambda i: (0,) * arr.ndim)
    acc = lambda shape: pl.BlockSpec(shape, lambda i: (0,) * len(shape))
    rev = lambda col: (lambda i: (nch - 1 - i, col))
    hblk = pl.BlockSpec((nq, t * 8, LANES), lambda i: (0, nch - 1 - i, 0))
    hprev = pl.BlockSpec(tile, lambda i: (0, jnp.maximum((nch - 1 - i) * t - 1, 0), 0))
    outs = _pcall(
        body, grid=(nch,),
        in_specs=[pl.BlockSpec((t, ds_), rev(0)), pl.BlockSpec((t, ds_), rev(ucol)), hblk, hblk, hprev, hprev,
                  full(bre), full(bim), full(cre), full(cim), full(lbr), full(lbi), full(dvec), ANY],
        out_specs=[pl.BlockSpec((t, ds_), rev(ucol)), acc(bre.shape), acc(bim.shape), acc(cre.shape), acc(cim.shape),
                   acc(tile), acc(tile), acc((8, ds_))],
        out_shape=[_sds(dz.shape, BF16), _sds(bre.shape, F32), _sds(bim.shape, F32), _sds(cre.shape, F32),
                   _sds(cim.shape, F32), _sds(tile, F32), _sds(tile, F32), _sds((8, ds_), F32)],
        scratch_shapes=[pltpu.VMEM((nq, t * 8, LANES), F32), pltpu.VMEM((nq, t * 8, LANES), F32),
                        pltpu.VMEM(tile, F32), pltpu.VMEM(tile, F32)],
        input_output_aliases={13: 0},
        compiler_params=_cparams(1), name="s5_bwd")(dys, z, hre, him, hre, him, bre, bim, cre, cim, lbr, lbi, dvec, dz)
    return outs


def _gate_a_bwd(dmix, ya, z, dz, s, d, tn, col0):
    tm = min(s, 512)

    def body(dm_ref, ya_ref, g_ref, dzin_ref, dz_ref):
        dm = dm_ref[...].astype(F32)
        sg = jax.nn.sigmoid(g_ref[...])
        dz_ref[...] = (dm * ya_ref[...] * sg * (1.0 - sg)).astype(BF16)

    blk = pl.BlockSpec((tm, tn), lambda i, j: (i, j))
    zblk = pl.BlockSpec((tm, tn), lambda i, j: (i, col0 + j))
    return _pcall(body, grid=(s // tm, d // tn), in_specs=[blk, blk, zblk, ANY], out_specs=zblk,
                  out_shape=_sds(dz.shape, BF16), input_output_aliases={3: 0},
                  compiler_params=_cparams(2), name="gate_a_bwd")(dmix, ya, z, dz)


def _gate_b_bwd(dmix, p, q, z, dz, s, d, tn, col0):
    tm = min(s, 512)

    def body(dm_ref, p_ref, q_ref, g_ref, dzin_ref, dz_ref, dp_ref, dq_ref):
        dm = dm_ref[...].astype(F32)
        sg = jax.nn.sigmoid(g_ref[...])
        sq = jax.nn.sigmoid(q_ref[...])
        pv = p_ref[...]
        dz_ref[...] = (dm * (pv * sq) * sg * (1.0 - sg)).astype(BF16)
        dyb = dm * sg
        dp_ref[...] = (dyb * sq).astype(BF16)
        dq_ref[...] = (dyb * pv * sq * (1.0 - sq)).astype(BF16)

    blk = pl.BlockSpec((tm, tn), lambda i, j: (i, j))
    zblk = pl.BlockSpec((tm, tn), lambda i, j: (i, col0 + j))
    return _pcall(body, grid=(s // tm, d // tn), in_specs=[blk, blk, blk, zblk, ANY], out_specs=[zblk, blk, blk],
                  out_shape=[_sds(dz.shape, BF16), _sds((s, d), BF16), _sds((s, d), BF16)],
                  input_output_aliases={4: 0}, compiler_params=_cparams(2), name="gate_b_bwd")(dmix, p, q, z, dz)


def _s5_disc(a_re, a_im, log_dt, b_re, b_im):
    dt = jnp.exp(log_dt)[:, None]
    lr = jnp.minimum(a_re, -1e-4)
    li = a_im
    mag = jnp.exp(lr * dt)
    lbr = mag * jnp.cos(li * dt)
    lbi = mag * jnp.sin(li * dt)
    zr, zi = lbr - 1.0, lbi
    den = lr * lr + li * li
    fr = (zr * lr + zi * li) / den
    fi = (zi * lr - zr * li) / den
    bbr = fr[..., None] * b_re - fi[..., None] * b_im
    bbi = fr[..., None] * b_im + fi[..., None] * b_re
    return lbr, lbi, bbr, bbi


def _s5_bmat(bb):
    g, p, h = bb.shape
    gpb = g // SCAN_ROWS
    eye = jnp.eye(gpb, dtype=F32)
    r = bb.reshape(SCAN_ROWS, gpb, p, h).transpose(0, 1, 3, 2)
    m = r[:, :, :, None, :] * eye[None, :, None, :, None]
    return m.reshape(SCAN_ROWS, gpb * h, gpb * p)


def _s5_bmat_diag(m, g, p, h):
    gpb = g // SCAN_ROWS
    eye = jnp.eye(gpb, dtype=F32)
    r = (m.reshape(SCAN_ROWS, gpb, h, gpb, p) * eye[None, :, None, :, None]).sum(axis=3)
    return r.transpose(0, 1, 3, 2).reshape(g, p, h)


def _s5_cmat(cc):
    g, h, p = cc.shape
    gpb = g // SCAN_ROWS
    eye = jnp.eye(gpb, dtype=F32)
    r = cc.reshape(SCAN_ROWS, gpb, h, p).transpose(0, 1, 3, 2)
    m = r[:, :, :, None, :] * eye[None, :, None, :, None]
    return m.reshape(SCAN_ROWS, gpb * p, gpb * h)


def _s5_cmat_diag(m, g, h, p):
    gpb = g // SCAN_ROWS
    eye = jnp.eye(gpb, dtype=F32)
    r = (m.reshape(SCAN_ROWS, gpb, p, gpb, h) * eye[None, :, None, :, None]).sum(axis=3)
    return r.transpose(0, 1, 3, 2).reshape(g, h, p)


PACK_W = 1024
PACK_TILE = 8 * PACK_W


def _pack(arrs, total_rows):
    parts = []
    for a in arrs:
        f = a.reshape(-1).astype(F32)
        pad = (-f.shape[0]) % PACK_TILE
        parts.append(jnp.pad(f, (0, pad)).reshape(-1, PACK_W))
    rows = sum(p.shape[0] for p in parts)
    if total_rows > rows:
        parts.append(jnp.zeros((total_rows - rows, PACK_W), F32))
    return jnp.concatenate(parts, axis=0)


def _unpack(buf, shapes):
    out, row = [], 0
    for shp in shapes:
        n = math.prod(shp)
        rows = -(-n // PACK_TILE) * 8
        out.append(buf[row:row + rows].reshape(-1)[:n].reshape(shp))
        row += rows
    return out


def _pack_rows(shapes):
    rows = sum(-(-math.prod(s) // PACK_TILE) * 8 for s in shapes)
    return -(-rows // 64) * 64


def kernel(x, w_in, conv_w, conv_b, rg_wa, rg_ba, rg_wx, rg_bx, rg_lambda, w_a_out, ssm_a_re, ssm_a_im, ssm_log_dt, ssm_b_re, ssm_b_im, ssm_c_re, ssm_c_im, ssm_d, glu_w, glu_v, w_out, ln1_g, ln1_b, mlp_w_up, mlp_b_up, mlp_w_down, mlp_b_down, ln2_g, ln2_b, loss_target, m_w_in, m_conv_w, m_conv_b, m_rg_wa, m_rg_ba, m_rg_wx, m_rg_bx, m_rg_lambda, m_w_a_out, m_ssm_a_re, m_ssm_a_im, m_ssm_log_dt, m_ssm_b_re, m_ssm_b_im, m_ssm_c_re, m_ssm_c_im, m_ssm_d, m_glu_w, m_glu_v, m_w_out, m_ln1_g, m_ln1_b, m_mlp_w_up, m_mlp_b_up, m_mlp_w_down, m_mlp_b_down, m_ln2_g, m_ln2_b, v_w_in, v_conv_w, v_conv_b, v_rg_wa, v_rg_ba, v_rg_wx, v_rg_bx, v_rg_lambda, v_w_a_out, v_ssm_a_re, v_ssm_a_im, v_ssm_log_dt, v_ssm_b_re, v_ssm_b_im, v_ssm_c_re, v_ssm_c_im, v_ssm_d, v_glu_w, v_glu_v, v_w_out, v_ln1_g, v_ln1_b, v_mlp_w_up, v_mlp_b_up, v_mlp_w_down, v_mlp_b_down, v_ln2_g, v_ln2_b):
    local = dict(locals())
    s, d = x.shape[1], x.shape[2]
    dr, ds_ = d, d // 2
    din4 = w_in.shape[2]
    din = N_CHIPS * din4
    df4 = mlp_w_up.shape[2]
    df = N_CHIPS * df4
    d4 = d // N_CHIPS
    nh, hd = rg_wa.shape[1], rg_wa.shape[2]
    grp, pst, gh = ssm_b_re.shape[1], ssm_b_re.shape[2], ssm_b_re.shape[3]
    tch = min(s, 256)
    tm = min(s, 1024)
    tmr = min(s, 256)
    xi, yi, ci = _me()
    c_arr = ci.reshape(1).astype(jnp.int32)
    chip_arr = (2 * xi + yi).reshape(1).astype(jnp.int32)

    x2 = x[0]
    tgt = loss_target[0]
    row = lambda a: a.reshape(1, -1)

    def gather(name, w):
        r, c = w.shape
        wb = _cast_bf16("cast_" + name, w)
        return _all_gather_chips("ag_" + name, wb.reshape(2, r // 2, c)).reshape(N_CHIPS, r, c)

    w_in_s = gather("w_in", w_in[0])
    w_a_out_f = gather("w_a_out", w_a_out[0]).reshape(dr, d)
    glu_w_s = gather("glu_w", glu_w[0])
    glu_v_s = gather("glu_v", glu_v[0])
    w_out_f = gather("w_out", w_out[0]).reshape(d, d)
    w_up_s = gather("mlp_w_up", mlp_w_up[0])
    w_down_f = gather("mlp_w_down", mlp_w_down[0]).reshape(df, d)

    cw_place = jnp.zeros((4, dr), F32)
    cw_place = lax.dynamic_update_slice(cw_place, conv_w[0] * (ci == 0).astype(F32), (0, (2 * xi + yi) * d4))
    cw_rows = max(8, (4 * dr) // (8 * PACK_W))
    cw_pad = jnp.zeros((8 * cw_rows * PACK_W,), F32).at[:4 * dr].set(cw_place.reshape(-1))
    cw_full = _all_reduce_small("ar_conv_w", cw_pad.reshape(8, cw_rows, PACK_W)).reshape(-1)[:4 * dr].reshape(4, dr)

    lam = rg_lambda[0]
    la = row(jax.nn.softplus(-lam))
    wa_b = rg_wa[0].astype(BF16)
    wx_b = rg_wx[0].astype(BF16)
    disc_in = (ssm_a_re[0], ssm_a_im[0], ssm_log_dt[0], ssm_b_re[0], ssm_b_im[0])
    (lbr, lbi, bbr, bbi), disc_vjp = jax.vjp(_s5_disc, *disc_in)
    bre_m, bim_m = _s5_bmat(bbr).astype(BF16), _s5_bmat(bbi).astype(BF16)
    cre_m, cim_m = _s5_cmat(ssm_c_re[0]).astype(BF16), _s5_cmat(ssm_c_im[0]).astype(BF16)
    to_tile = lambda a: a.reshape(SCAN_ROWS, -1, LANES).transpose(1, 0, 2)
    from_tile = lambda a: a.transpose(1, 0, 2).reshape(grp, pst)
    lbr_t, lbi_t = to_tile(lbr), to_tile(lbi)
    dvec = row(ssm_d[0])

    tn_in = d // 8
    n4 = din4 // tn_in
    z, = _mm("in_proj", x2, w_in_s,
             pl.BlockSpec((tm, d), lambda i, j, k: (i, 0)),
             pl.BlockSpec((None, d, tn_in), lambda i, j, k: (j // n4, 0, j % n4)),
             NN, (s // tm, din // tn_in, 1), [_sds((s, din), F32)],
             [pl.BlockSpec((tm, tn_in), lambda i, j, k: (i, j))], _store())
    ucol = (2 * dr) // ds_
    hsave, hg = _rglru_fwd(z, cw_full, row(conv_b[0]), wa_b, row(rg_ba[0]), wx_b, row(rg_bx[0]), la, s, dr, tch)
    ys, yg, hre, him = _s5_fwd(z, bre_m, bim_m, cre_m, cim_m, lbr_t, lbi_t, dvec, s, ds_, tch, ucol)

    tn4 = d4
    ya, = _mm("ya", hg, w_a_out_f,
              pl.BlockSpec((tm, dr), lambda i, j, k: (i, 0)), pl.BlockSpec((dr, tn4), lambda i, j, k: (0, j)),
              NN, (s // tm, d // tn4, 1), [_sds((s, d), F32)], [pl.BlockSpec((tm, tn4), lambda i, j, k: (i, j))], _store())
    glu_spec = pl.BlockSpec((None, ds_, tn4), lambda i, j, k: (j, 0, 0))
    pp, = _mm("glu_p", yg, glu_w_s, pl.BlockSpec((tm, ds_), lambda i, j, k: (i, 0)), glu_spec,
              NN, (s // tm, N_CHIPS, 1), [_sds((s, d), F32)], [pl.BlockSpec((tm, tn4), lambda i, j, k: (i, j))], _store())
    col_ga = (2 * dr + ds_) // tn4
    col_gb = col_ga + d // tn4

    def mix_epi(acc, ex, outs):
        p_ref, ya_ref, ga_ref, gb_ref = ex
        outs[0][...] = acc
        mix = jax.nn.sigmoid(ga_ref[...]) * ya_ref[...] + jax.nn.sigmoid(gb_ref[...]) * (p_ref[...] * jax.nn.sigmoid(acc))
        outs[1][...] = mix.astype(BF16)

    blk4 = pl.BlockSpec((tm, tn4), lambda i, j, k: (i, j))
    qq, mixb = _mm("glu_q_mix", yg, glu_v_s, pl.BlockSpec((tm, ds_), lambda i, j, k: (i, 0)), glu_spec,
                   NN, (s // tm, N_CHIPS, 1), [_sds((s, d), F32), _sds((s, d), BF16)], [blk4, blk4], mix_epi,
                   extras=(pp, ya, z, z),
                   extra_specs=(blk4, blk4, pl.BlockSpec((tm, tn4), lambda i, j, k: (i, col_ga + j)),
                                pl.BlockSpec((tm, tn4), lambda i, j, k: (i, col_gb + j))))

    g1, b1, g2, b2 = row(ln1_g[0]), row(ln1_b[0]), row(ln2_g[0]), row(ln2_b[0])
    vec = lambda n: pl.BlockSpec((1, n), lambda i, j, k: (0, 0))
    rowblk = pl.BlockSpec((tmr, d), lambda i, j, k: (i, 0))
    colblk1 = pl.BlockSpec((tmr, 1), lambda i, j, k: (i, 0))
    part_blk = pl.BlockSpec((8, d), lambda i, j, k: (i, 0))

    def ln1_epi(acc, ex, outs):
        x_ref, g_ref, b_ref = ex
        r1 = ALPHA * x_ref[...] + acc
        mu = jnp.mean(r1, axis=-1, keepdims=True)
        cen = r1 - mu
        var = jnp.mean(cen * cen, axis=-1, keepdims=True)
        rstd = lax.rsqrt(var + LN_EPS)
        xh = cen * rstd
        outs[0][...] = xh
        outs[1][...] = (xh * g_ref[...] + b_ref[...]).astype(BF16)
        outs[2][...] = rstd

    xhat1, x1b, rstd1 = _mm("out_proj_ln1", mixb, w_out_f, rowblk, pl.BlockSpec((d, d), lambda i, j, k: (0, 0)),
                            NN, (s // tmr, 1, 1), [_sds((s, d), F32), _sds((s, d), BF16), _sds((s, 1), F32)],
                            [rowblk, rowblk, colblk1], ln1_epi, extras=(x2, g1, b1), extra_specs=(rowblk, vec(d), vec(d)))

    tnf = min(df4, 1024)
    nf4 = df4 // tnf

    def up_epi(acc, ex, outs):
        hp = acc + ex[0][...]
        rl = jnp.maximum(hp, 0.0)
        outs[0][...] = (rl * rl).astype(BF16)
        outs[1][...] = rl.astype(BF16)

    fblk = pl.BlockSpec((tm, tnf), lambda i, j, k: (i, j))
    hact, hrelu = _mm("mlp_up", x1b, w_up_s, pl.BlockSpec((tm, d), lambda i, j, k: (i, 0)),
                      pl.BlockSpec((None, d, tnf), lambda i, j, k: (j // nf4, 0, j % nf4)),
                      NN, (s // tm, df // tnf, 1), [_sds((s, df), BF16), _sds((s, df), BF16)], [fblk, fblk], up_epi,
                      extras=(row(mlp_b_up[0]),), extra_specs=(pl.BlockSpec((1, tnf), lambda i, j, k: (0, j)),))

    tkd = min(df, 1024)

    def down_epi(acc, ex, outs):
        xh1_ref, t_ref, g1_ref, b1_ref, g2_ref, b2_ref, bd_ref = ex
        x1 = xh1_ref[...] * g1_ref[...] + b1_ref[...]
        r2 = ALPHA * x1 + (acc + bd_ref[...])
        mu = jnp.mean(r2, axis=-1, keepdims=True)
        cen = r2 - mu
        var = jnp.mean(cen * cen, axis=-1, keepdims=True)
        rstd = lax.rsqrt(var + LN_EPS)
        xh2 = cen * rstd
        err = (xh2 * g2_ref[...] + b2_ref[...]) - t_ref[...]
        tot = 0.5 * jnp.sum(jnp.mean(err * err, axis=-1, keepdims=True))
        rows_i = lax.broadcasted_iota(jnp.int32, (8, 128), 0)
        cols_i = lax.broadcasted_iota(jnp.int32, (8, 128), 1)
        outs[5][...] = jnp.where((rows_i == 0) & (cols_i == 0), tot, 0.0)
        dy = err * (1.0 / d)
        outs[2][...] = _rows_part(dy * xh2)
        outs[3][...] = _rows_part(dy)
        dxh = dy * g2_ref[...]
        m1 = jnp.mean(dxh, axis=-1, keepdims=True)
        m2 = jnp.mean(dxh * xh2, axis=-1, keepdims=True)
        dr2 = rstd * (dxh - m1 - xh2 * m2)
        outs[0][...] = dr2
        outs[1][...] = dr2.astype(BF16)
        outs[4][...] = _rows_part(dr2)

    nrb = s // tmr
    dr2, dr2b, dg2p, db2p, dbdp, lossp = _mm(
        "mlp_down_ln2_loss", hact, w_down_f, pl.BlockSpec((tmr, tkd), lambda i, j, k: (i, k)),
        pl.BlockSpec((tkd, d), lambda i, j, k: (k, 0)), NN, (nrb, 1, df // tkd),
        [_sds((s, d), F32), _sds((s, d), BF16), _sds((nrb * 8, d), F32), _sds((nrb * 8, d), F32),
         _sds((nrb * 8, d), F32), _sds((nrb * 8, 128), F32)],
        [rowblk, rowblk, part_blk, part_blk, part_blk, pl.BlockSpec((8, 128), lambda i, j, k: (i, 0))], down_epi,
        extras=(xhat1, tgt, g1, b1, g2, b2, row(mlp_b_down[0])),
        extra_specs=(rowblk, rowblk, vec(d), vec(d), vec(d), vec(d), vec(d)), acc_shape=(tmr, d))

    def dh_epi(acc, ex, outs):
        dh = acc * (2.0 * ex[0][...].astype(F32))
        outs[0][...] = dh.astype(BF16)
        outs[1][...] = _rows_part(dh)

    ntb = s // tm
    dhpre, dbup_p = _mm("mlp_down_bwd", dr2b, w_down_f, pl.BlockSpec((tm, d), lambda i, j, k: (i, 0)),
                        pl.BlockSpec((tnf, d), lambda i, j, k: (j, 0)), NT, (ntb, df // tnf, 1),
                        [_sds((s, df), BF16), _sds((ntb * 8, df), F32)],
                        [fblk, pl.BlockSpec((8, tnf), lambda i, j, k: (i, j))], dh_epi,
                        extras=(hrelu,), extra_specs=(fblk,))

    tt = min(s, 512)
    ntt = s // tt
    tkk = min(d, 1024)
    g_down, = _mm("grad_w_down", hact, dr2b, pl.BlockSpec((tt, tkk), lambda i, j, k: (k, i)),
                  pl.BlockSpec((tt, tkk), lambda i, j, k: (k, j)), TN, (df // tkk, d // tkk, ntt),
                  [_sds((df, d), F32)], [pl.BlockSpec((tkk, tkk), lambda i, j, k: (i, j))], _store(), acc_shape=(tkk, tkk))
    g_up, = _mm("grad_w_up", x1b, dhpre, pl.BlockSpec((tt, tkk), lambda i, j, k: (k, i)),
                pl.BlockSpec((tt, tnf), lambda i, j, k: (k, j)), TN, (d // tkk, df // tnf, ntt),
                [_sds((N_CHIPS, d, df4), F32)],
                [pl.BlockSpec((None, tkk, tnf), lambda i, j, k: (j // nf4, i, j % nf4))], _store(), acc_shape=(tkk, tnf))

    def ln1_bwd_epi(acc, ex, outs):
        dr2_ref, xh_ref, rs_ref, g_ref = ex
        dx1 = ALPHA * dr2_ref[...] + acc
        xh = xh_ref[...]
        outs[2][...] = _rows_part(dx1 * xh)
        outs[3][...] = _rows_part(dx1)
        dxh = dx1 * g_ref[...]
        m1 = jnp.mean(dxh, axis=-1, keepdims=True)
        m2 = jnp.mean(dxh * xh, axis=-1, keepdims=True)
        dr1 = rs_ref[...] * (dxh - m1 - xh * m2)
        outs[0][...] = dr1
        outs[1][...] = dr1.astype(BF16)

    dr1, dr1b, dg1p, db1p = _mm(
        "mlp_up_bwd_ln1_bwd", dhpre, w_up_s, pl.BlockSpec((tmr, tnf), lambda i, j, k: (i, k)),
        pl.BlockSpec((None, d, tnf), lambda i, j, k: (k // nf4, 0, k % nf4)), NT, (nrb, 1, df // tnf),
        [_sds((s, d), F32), _sds((s, d), BF16), _sds((nrb * 8, d), F32), _sds((nrb * 8, d), F32)],
        [rowblk, rowblk, part_blk, part_blk], ln1_bwd_epi,
        extras=(dr2, xhat1, rstd1, g1), extra_specs=(rowblk, rowblk, colblk1, vec(d)), acc_shape=(tmr, d))

    dmix, = _mm("out_proj_bwd", dr1b, w_out_f, pl.BlockSpec((tm, d), lambda i, j, k: (i, 0)),
                pl.BlockSpec((tn4, d), lambda i, j, k: (j, 0)), NT, (ntb, d // tn4, 1),
                [_sds((s, d), BF16)], [blk4], _store())
    g_out, = _mm("grad_w_out", mixb, dr1b, pl.BlockSpec((tt, tkk), lambda i, j, k: (k, i)),
                 pl.BlockSpec((tt, tkk), lambda i, j, k: (k, j)), TN, (d // tkk, d // tkk, ntt),
                 [_sds((d, d), F32)], [pl.BlockSpec((tkk, tkk), lambda i, j, k: (i, j))], _store(), acc_shape=(tkk, tkk))

    def dya_body(dm_ref, g_ref, o_ref):
        o_ref[...] = (dm_ref[...].astype(F32) * jax.nn.sigmoid(g_ref[...])).astype(BF16)

    tme = min(s, 512)
    eblk = pl.BlockSpec((tme, tn4), lambda i, j: (i, j))
    dya = _pcall(dya_body, grid=(s // tme, d // tn4),
                 in_specs=[eblk, pl.BlockSpec((tme, tn4), lambda i, j: (i, col_ga + j))], out_specs=eblk,
                 out_shape=_sds((s, d), BF16), compiler_params=_cparams(2), name="dya")(dmix, z)
    g_a_out, = _mm("grad_w_a_out", hg, dya, pl.BlockSpec((tt, tkk), lambda i, j, k: (k, i)),
                   pl.BlockSpec((tt, tkk), lambda i, j, k: (k, j)), TN, (dr // tkk, d // tkk, ntt),
                   [_sds((dr, d), F32)], [pl.BlockSpec((tkk, tkk), lambda i, j, k: (i, j))], _store(), acc_shape=(tkk, tkk))
    dhg, = _mm("a_out_bwd", dya, w_a_out_f, pl.BlockSpec((tm, d), lambda i, j, k: (i, 0)),
               pl.BlockSpec((tn4, d), lambda i, j, k: (j, 0)), NT, (ntb, dr // tn4, 1),
               [_sds((s, dr), F32)], [blk4], _store())
    dz, dcw_p, dcb_p, dwa, dwx, dba_p, dbx_p, dla_p = _rglru_bwd(
        dhg, z, hsave, cw_full, row(conv_b[0]), wa_b, row(rg_ba[0]), wx_b, row(rg_bx[0]), la, s, dr, din, tch)
    dz = _gate_a_bwd(dmix, ya, z, dz, s, d, tn4, col_ga)
    dz, dpb, dqb = _gate_b_bwd(dmix, pp, qq, z, dz, s, d, tn4, col_gb)

    g_glu_w, = _mm("grad_glu_w", yg, dpb, pl.BlockSpec((tt, ds_), lambda i, j, k: (k, 0)),
                   pl.BlockSpec((tt, tn4), lambda i, j, k: (k, j)), TN, (1, N_CHIPS, ntt),
                   [_sds((N_CHIPS, ds_, d4), F32)], [pl.BlockSpec((None, ds_, tn4), lambda i, j, k: (j, 0, 0))],
                   _store(), acc_shape=(ds_, tn4))
    g_glu_v, = _mm("grad_glu_v", yg, dqb, pl.BlockSpec((tt, ds_), lambda i, j, k: (k, 0)),
                   pl.BlockSpec((tt, tn4), lambda i, j, k: (k, j)), TN, (1, N_CHIPS, ntt),
                   [_sds((N_CHIPS, ds_, d4), F32)], [pl.BlockSpec((None, ds_, tn4), lambda i, j, k: (j, 0, 0))],
                   _store(), acc_shape=(ds_, tn4))
    sblk = pl.BlockSpec((tm, ds_), lambda i, j, k: (i, 0))
    glu_bspec = pl.BlockSpec((None, ds_, tn4), lambda i, j, k: (k, 0, 0))
    dyg_p, = _mm("glu_w_bwd", dpb, glu_w_s, pl.BlockSpec((tm, tn4), lambda i, j, k: (i, k)), glu_bspec,
                 NT, (ntb, 1, N_CHIPS), [_sds((s, ds_), F32)], [sblk], _store(), acc_shape=(tm, ds_))

    def dys_epi(acc, ex, outs):
        outs[0][...] = (acc + ex[0][...]) * _gelu_grad(ex[1][...])

    dys, = _mm("glu_v_bwd_gelu_bwd", dqb, glu_v_s, pl.BlockSpec((tm, tn4), lambda i, j, k: (i, k)), glu_bspec,
               NT, (ntb, 1, N_CHIPS), [_sds((s, ds_), F32)], [sblk], dys_epi,
               extras=(dyg_p, ys), extra_specs=(sblk, sblk), acc_shape=(tm, ds_))
    dz, dbre, dbim, dcre, dcim, dlr, dli, ddp = _s5_bwd(dys, z, hre, him, bre_m, bim_m, cre_m, cim_m, lbr_t, lbi_t,
                                                      dvec, dz, s, ds_, tch, ucol)

    tni = din4 // 3
    ni4 = 3
    g_in, = _mm("grad_w_in", x2, dz, pl.BlockSpec((tt, tkk), lambda i, j, k: (k, i)),
                pl.BlockSpec((tt, tni), lambda i, j, k: (k, j)), TN, (d // tkk, din // tni, ntt),
                [_sds((N_CHIPS, d, din4), F32)],
                [pl.BlockSpec((None, tkk, tni), lambda i, j, k: (j // ni4, i, j % ni4))], _store(), acc_shape=(tkk, tni))

    def dx_epi(acc, ex, outs):
        outs[0][...] = ALPHA * ex[0][...] + acc

    grad_x, = _mm("in_proj_bwd", dz, w_in_s, pl.BlockSpec((tmr, tni), lambda i, j, k: (i, k)),
                  pl.BlockSpec((None, d, tni), lambda i, j, k: (k // ni4, 0, k % ni4)), NT, (nrb, 1, din // tni),
                  [_sds((s, d), F32)], [rowblk], dx_epi, extras=(dr1,), extra_specs=(rowblk,), acc_shape=(tmr, d))

    big = [("w_in", g_in), ("w_a_out", g_a_out.reshape(N_CHIPS, dr // N_CHIPS, d)), ("glu_w", g_glu_w),
           ("glu_v", g_glu_v), ("w_out", g_out.reshape(N_CHIPS, d4, d)), ("mlp_w_up", g_up),
           ("mlp_w_down", g_down.reshape(N_CHIPS, df4, d))]
    g6 = [g.reshape(N_CHIPS, 2, g.shape[1] // 2, g.shape[2]) for _, g in big]
    recv1 = _rs_to_sibling("rs_sibling", g6)
    hsum = [_add_half("rs_add_half_" + n, g, r, c_arr) for (n, _), g, r in zip(big, g6, recv1)]
    recv2 = _rs_to_chips("rs_chips", hsum)
    rsum = [_add_chips("rs_add_chips_" + n, h, r, chip_arr) for (n, _), h, r in zip(big, hsum, recv2)]
    joined = _rs_join("rs_join", rsum)
    res = {}
    for (n, _), gj in zip(big, joined):
        w = local[n][0]
        g = gj.reshape(w.shape)
        dl, nm, nv = _adamw("adamw_" + n, w, g, local["m_" + n][0], local["v_" + n][0])
        res[n] = (g[None], dl[None], nm[None], nv[None])

    s8 = lambda p: p.sum(axis=0)
    red_names = ["loss", "conv_w", "conv_b", "rg_wa", "rg_ba", "rg_wx", "rg_bx", "dla", "lbr", "lbi", "bbr", "bbi",
                 "c_re", "c_im", "ssm_d", "ln1_g", "ln1_b", "mlp_b_up", "mlp_b_down", "ln2_g", "ln2_b"]
    red_vals = [lossp.sum().reshape(1), dcw_p.reshape(4, 8, dr).sum(axis=1), s8(dcb_p), dwa, s8(dba_p), dwx, s8(dbx_p),
                s8(dla_p), from_tile(dlr), from_tile(dli), _s5_bmat_diag(dbre, grp, pst, gh),
                _s5_bmat_diag(dbim, grp, pst, gh), _s5_cmat_diag(dcre, grp, gh, pst), _s5_cmat_diag(dcim, grp, gh, pst),
                s8(ddp).reshape(grp, gh), s8(dg1p), s8(db1p), s8(dbup_p), s8(dbdp), s8(dg2p), s8(db2p)]
    red_shapes = [v.shape for v in red_vals]
    rows_r = _pack_rows(red_shapes)
    red = _all_reduce_small("ar_small", _pack(red_vals, rows_r).reshape(8, rows_r // 8, PACK_W))
    rv = dict(zip(red_names, _unpack(red.reshape(rows_r, PACK_W), red_shapes)))
    loss = rv["loss"][0]
    d_are, d_aim, d_ldt, d_bre, d_bim = disc_vjp((rv["lbr"], rv["lbi"], rv["bbr"], rv["bbi"]))
    small_g = {"conv_w": rv["conv_w"], "conv_b": rv["conv_b"], "rg_wa": rv["rg_wa"], "rg_ba": rv["rg_ba"].reshape(nh, hd),
               "rg_wx": rv["rg_wx"], "rg_bx": rv["rg_bx"].reshape(nh, hd), "rg_lambda": rv["dla"] * (-jax.nn.sigmoid(-lam)),
               "ssm_a_re": d_are, "ssm_a_im": d_aim, "ssm_log_dt": d_ldt, "ssm_b_re": d_bre, "ssm_b_im": d_bim,
               "ssm_c_re": rv["c_re"], "ssm_c_im": rv["c_im"], "ssm_d": rv["ssm_d"], "ln1_g": rv["ln1_g"],
               "ln1_b": rv["ln1_b"], "mlp_b_up": rv["mlp_b_up"], "mlp_b_down": rv["mlp_b_down"],
               "ln2_g": rv["ln2_g"], "ln2_b": rv["ln2_b"]}
    small_names = list(small_g)
    col0 = (2 * xi + yi) * d4

    def placed(n, a):
        if n != "conv_w":
            return a[0]
        return lax.dynamic_update_slice(jnp.zeros((4, dr), F32), a[0], (0, col0))

    sm_shapes = [small_g[n].shape for n in small_names]
    rows_s = _pack_rows(sm_shapes)
    packs = [_pack([small_g[n] for n in small_names], rows_s)]
    for pre in ("", "m_", "v_"):
        packs.append(_pack([placed(n, local[pre + n]) for n in small_names], rows_s))
    dl_p, nm_p, nv_p = _adamw("adamw_small", packs[1], packs[0], packs[2], packs[3])
    dl_s = dict(zip(small_names, _unpack(dl_p, sm_shapes)))
    nm_s = dict(zip(small_names, _unpack(nm_p, sm_shapes)))
    nv_s = dict(zip(small_names, _unpack(nv_p, sm_shapes)))
    for n in small_names:
        tup = (small_g[n], dl_s[n], nm_s[n], nv_s[n])
        if n == "conv_w":
            tup = tuple(lax.dynamic_slice(a, (0, col0), (4, d4)) for a in tup)
        res[n] = tuple(a.reshape(local[n].shape) for a in tup)

    order = ["w_in", "conv_w", "conv_b", "rg_wa", "rg_ba", "rg_wx", "rg_bx", "rg_lambda", "w_a_out", "ssm_a_re",
             "ssm_a_im", "ssm_log_dt", "ssm_b_re", "ssm_b_im", "ssm_c_re", "ssm_c_im", "ssm_d", "glu_w", "glu_v",
             "w_out", "ln1_g", "ln1_b", "mlp_w_up", "mlp_b_up", "mlp_w_down", "mlp_b_down", "ln2_g", "ln2_b"]
    outs = [loss, grad_x[None]]
    for part in range(4):
        outs += [res[n][part] for n in order]
    return tuple(outs)
```

```python
import functools
import math

import jax
import jax.numpy as jnp
from jax import lax
from jax.experimental import pallas as pl
from jax.experimental.pallas import tpu as pltpu

F32 = jnp.float32
BF16 = jnp.bfloat16
MESH = pl.DeviceIdType.MESH
ANY = pl.BlockSpec(memory_space=pl.ANY)

ALPHA = 2.0 ** 0.25
LN_EPS = 1e-5
RG_C = 8.0
ADAM_LR, ADAM_B1, ADAM_B2, ADAM_EPS, ADAM_WD, ADAM_STEP = 0.001, 0.9, 0.999, 1e-08, 0.01, 10
GELU_K0 = math.sqrt(2.0 / math.pi)
GELU_K1 = 0.044715
VMEM_LIMIT = 56 * 1024 * 1024
N_CHIPS = 4
SCAN_ROWS = 8


def _pcall(body, **kw):
    return pl.pallas_call(body, **kw)


def _cparams(n_grid):
    return pltpu.CompilerParams(dimension_semantics=("arbitrary",) * n_grid, vmem_limit_bytes=VMEM_LIMIT)


def _gelu(x):
    return 0.5 * x * (1.0 + jnp.tanh(GELU_K0 * (x + GELU_K1 * x * x * x)))


def _gelu_grad(x):
    t = jnp.tanh(GELU_K0 * (x + GELU_K1 * x * x * x))
    return 0.5 * (1.0 + t) + 0.5 * x * (1.0 - t * t) * GELU_K0 * (1.0 + 3.0 * GELU_K1 * x * x)


def _neg_expm1(x):
    series = x * (1.0 + x * (0.5 + x * (1.0 / 6.0 + x * (1.0 / 24.0))))
    return -jnp.where(x > -0.03, series, jnp.exp(x) - 1.0)


def _rows_part(v):
    m, n = v.shape
    return v.reshape(m // 8, 8, n).sum(axis=0)


def _dot(a, b):
    return jnp.dot(a, b, preferred_element_type=F32)


def _dot_nt(a, b):
    return lax.dot_general(a, b, (((1,), (1,)), ((), ())), preferred_element_type=F32)


def _dot_tn(a, b):
    return lax.dot_general(a, b, (((0,), (0,)), ((), ())), preferred_element_type=F32)


NN = (((1,), (0,)), ((), ()))
NT = (((1,), (1,)), ((), ()))
TN = (((0,), (0,)), ((), ()))


def _mm(name, a, b, a_spec, b_spec, contract, grid, out_shape, out_specs, epilogue,
        extras=(), extra_specs=(), acc_shape=None, aliases=None):
    nk = grid[2]
    ne = len(extras)
    n_out = len(out_shape)

    def body(*refs):
        a_ref, b_ref = refs[0], refs[1]
        ex = refs[2:2 + ne]
        outs = refs[2 + ne:2 + ne + n_out]
        prod = lax.dot_general(a_ref[...].astype(BF16), b_ref[...].astype(BF16), contract,
                               preferred_element_type=F32)
        if nk == 1:
            epilogue(prod, ex, outs)
        else:
            acc = refs[2 + ne + n_out]
            k = pl.program_id(2)

            @pl.when(k == 0)
            def _():
                acc[...] = prod

            @pl.when(k > 0)
            def _():
                acc[...] += prod

            @pl.when(k == nk - 1)
            def _():
                epilogue(acc[...], ex, outs)

    scratch = [pltpu.VMEM(acc_shape, F32)] if nk > 1 else []
    return _pcall(body, grid=grid, in_specs=[a_spec, b_spec, *extra_specs], out_specs=list(out_specs),
                  out_shape=list(out_shape), scratch_shapes=scratch, input_output_aliases=aliases or {},
                  compiler_params=_cparams(3), name=name)(a, b, *extras)


def _store(dtype=None):
    def epi(acc, ex, outs):
        outs[0][...] = acc.astype(outs[0].dtype)
    return epi


def _store_both(acc, ex, outs):
    outs[0][...] = acc
    outs[1][...] = acc.astype(BF16)


def _sds(shape, dtype):
    return jax.ShapeDtypeStruct(tuple(shape), dtype)


def _row_tile(rows, cols, budget=1 << 20):
    t = rows
    while t * cols * 4 > budget and t % 16 == 0:
        t //= 2
    return t


def _cast_bf16(name, w):
    r, c = w.shape
    tr = _row_tile(r, c)

    def body(w_ref, o_ref):
        o_ref[...] = w_ref[...].astype(BF16)

    return _pcall(body, grid=(r // tr,), in_specs=[pl.BlockSpec((tr, c), lambda i: (i, 0))],
                  out_specs=pl.BlockSpec((tr, c), lambda i: (i, 0)), out_shape=_sds((r, c), BF16),
                  compiler_params=_cparams(1), name=name)(w)


def _adamw(name, w, g, m, v):
    r, c = w.shape
    tr = _row_tile(r, c)

    def body(w_ref, g_ref, m_ref, v_ref, d_ref, nm_ref, nv_ref):
        gg = g_ref[...]
        nm = ADAM_B1 * m_ref[...] + (1.0 - ADAM_B1) * gg
        nv = ADAM_B2 * v_ref[...] + (1.0 - ADAM_B2) * (gg * gg)
        m_hat = nm / (1.0 - ADAM_B1 ** ADAM_STEP)
        v_hat = nv / (1.0 - ADAM_B2 ** ADAM_STEP)
        d_ref[...] = -ADAM_LR * (m_hat / (jnp.sqrt(v_hat) + ADAM_EPS) + ADAM_WD * w_ref[...])
        nm_ref[...] = nm
        nv_ref[...] = nv

    spec = pl.BlockSpec((tr, c), lambda i: (i, 0))
    return _pcall(body, grid=(r // tr,), in_specs=[spec] * 4, out_specs=[spec] * 3,
                  out_shape=[_sds((r, c), F32)] * 3, compiler_params=_cparams(1), name=name)(w, g, m, v)


def _add_half(name, g, recv, c_arr):
    _, _, rh, c = g.shape
    tr = _row_tile(rh, c)

    def body(c_ref, g_ref, r_ref, o_ref, ob_ref):
        h = g_ref[...] + r_ref[...].astype(F32)
        o_ref[...] = h
        ob_ref[...] = h.astype(BF16)

    spec3 = pl.BlockSpec((None, tr, c), lambda s, r, cc: (s, r, 0))
    gs = pltpu.PrefetchScalarGridSpec(
        num_scalar_prefetch=1, grid=(N_CHIPS, rh // tr),
        in_specs=[pl.BlockSpec((None, None, tr, c), lambda s, r, cc: (s, cc[0], r, 0)), spec3],
        out_specs=[spec3, spec3])
    return _pcall(body, grid_spec=gs, out_shape=[_sds((N_CHIPS, rh, c), F32), _sds((N_CHIPS, rh, c), BF16)],
                  compiler_params=_cparams(2), name=name)(c_arr, g, recv)


def _add_chips(name, h, recv, chip_arr):
    _, rh, c = h.shape
    tr = _row_tile(rh, c)

    def body(k_ref, h_ref, r0, r1, r2, o_ref):
        o_ref[...] = ((h_ref[...] + r0[...].astype(F32)) + r1[...].astype(F32)) + r2[...].astype(F32)

    def rspec(k):
        return pl.BlockSpec((None, tr, c), lambda r, kk, k=k: (k, r, 0))

    gs = pltpu.PrefetchScalarGridSpec(
        num_scalar_prefetch=1, grid=(rh // tr,),
        in_specs=[pl.BlockSpec((None, tr, c), lambda r, kk: (kk[0], r, 0)), rspec(0), rspec(1), rspec(2)],
        out_specs=pl.BlockSpec((tr, c), lambda r, kk: (r, 0)))
    return _pcall(body, grid_spec=gs, out_shape=_sds((rh, c), F32),
                  compiler_params=_cparams(1), name=name)(chip_arr, h, recv, recv, recv)


def _me():
    return lax.axis_index("x"), lax.axis_index("y"), lax.axis_index("c")


def _peer_chip(x, y, k):
    px = (x + (k >> 1)) % 2
    py = (y + (k & 1)) % 2
    return px, py, 2 * px + py


def _all_gather_chips(name, w2):
    _, rh, c_ = w2.shape

    def body(in_ref, out_ref, send_sems, recv_sems, loc_sem):
        x, y, c = _me()
        chip = 2 * x + y
        sib = (x, y, 1 - c)
        loc = pltpu.make_async_copy(in_ref, out_ref.at[chip], loc_sem)
        loc.start()
        started = []
        for k in (1, 2, 3):
            px, py, _ = _peer_chip(x, y, k)
            cp = pltpu.make_async_remote_copy(src_ref=in_ref.at[c], dst_ref=out_ref.at[chip, c],
                                              send_sem=send_sems.at[k - 1], recv_sem=recv_sems.at[k - 1],
                                              device_id=(px, py, c), device_id_type=MESH)
            cp.start()
            started.append(cp)
        for k in (1, 2, 3):
            _, _, pchip = _peer_chip(x, y, k)
            pltpu.make_async_remote_copy(src_ref=in_ref.at[c], dst_ref=out_ref.at[pchip, c],
                                         send_sem=send_sems.at[k - 1], recv_sem=recv_sems.at[k - 1],
                                         device_id=sib, device_id_type=MESH).wait_recv()
            fw = pltpu.make_async_remote_copy(src_ref=out_ref.at[pchip, c], dst_ref=out_ref.at[pchip, c],
                                              send_sem=send_sems.at[2 + k], recv_sem=recv_sems.at[2 + k],
                                              device_id=sib, device_id_type=MESH)
            fw.start()
            started.append(fw)
        for k in (1, 2, 3):
            _, _, pchip = _peer_chip(x, y, k)
            pltpu.make_async_remote_copy(src_ref=in_ref.at[c], dst_ref=out_ref.at[pchip, 1 - c],
                                         send_sem=send_sems.at[2 + k], recv_sem=recv_sems.at[2 + k],
                                         device_id=sib, device_id_type=MESH).wait_recv()
        for cp in started:
            cp.wait_send()
        loc.wait()

    return _pcall(body, in_specs=[ANY], out_specs=ANY, out_shape=_sds((N_CHIPS, 2, rh, c_), w2.dtype),
                  scratch_shapes=[pltpu.SemaphoreType.DMA((6,)), pltpu.SemaphoreType.DMA((6,)),
                                  pltpu.SemaphoreType.DMA], name=name)(w2)


def _rs_to_sibling(name, gs):
    n = len(gs)

    def body(*refs):
        ins, outs = refs[:n], refs[n:2 * n]
        send_sems, recv_sems = refs[2 * n], refs[2 * n + 1]
        x, y, c = _me()
        sib = (x, y, 1 - c)
        cps = []
        for a in range(n):
            cp = pltpu.make_async_remote_copy(src_ref=ins[a].at[:, 1 - c], dst_ref=outs[a],
                                              send_sem=send_sems.at[a], recv_sem=recv_sems.at[a],
                                              device_id=sib, device_id_type=MESH)
            cp.start()
            cps.append(cp)
        for cp in cps:
            cp.wait_recv()
        for cp in cps:
            cp.wait_send()

    shapes = [_sds((g.shape[0], g.shape[2], g.shape[3]), g.dtype) for g in gs]
    return _pcall(body, in_specs=[ANY] * n, out_specs=[ANY] * n, out_shape=shapes,
                  scratch_shapes=[pltpu.SemaphoreType.DMA((n,)), pltpu.SemaphoreType.DMA((n,))], name=name)(*gs)


def _rs_to_chips(name, hs):
    n = len(hs)

    def body(*refs):
        ins, outs = refs[:n], refs[n:2 * n]
        send_sems, recv_sems = refs[2 * n], refs[2 * n + 1]
        x, y, c = _me()
        cps = []
        for a in range(n):
            for k in (1, 2, 3):
                px, py, pchip = _peer_chip(x, y, k)
                cp = pltpu.make_async_remote_copy(src_ref=ins[a].at[pchip], dst_ref=outs[a].at[k - 1],
                                                  send_sem=send_sems.at[a, k - 1], recv_sem=recv_sems.at[a, k - 1],
                                                  device_id=(px, py, c), device_id_type=MESH)
                cp.start()
                cps.append(cp)
        for cp in cps:
            cp.wait_recv()
        for cp in cps:
            cp.wait_send()

    shapes = [_sds((3, h.shape[1], h.shape[2]), h.dtype) for h in hs]
    return _pcall(body, in_specs=[ANY] * n, out_specs=[ANY] * n, out_shape=shapes,
                  scratch_shapes=[pltpu.SemaphoreType.DMA((n, 3)), pltpu.SemaphoreType.DMA((n, 3))], name=name)(*hs)


def _rs_join(name, rs):
    n = len(rs)

    def body(*refs):
        ins, outs = refs[:n], refs[n:2 * n]
        send_sems, recv_sems, loc_sems = refs[2 * n], refs[2 * n + 1], refs[2 * n + 2]
        x, y, c = _me()
        sib = (x, y, 1 - c)
        cps, locs = [], []
        for a in range(n):
            lc = pltpu.make_async_copy(ins[a], outs[a].at[c], loc_sems.at[a])
            lc.start()
            locs.append(lc)
            cp = pltpu.make_async_remote_copy(src_ref=ins[a], dst_ref=outs[a].at[c],
                                              send_sem=send_sems.at[a], recv_sem=recv_sems.at[a],
                                              device_id=sib, device_id_type=MESH)
            cp.start()
            cps.append(cp)
        for a in range(n):
            pltpu.make_async_remote_copy(src_ref=ins[a], dst_ref=outs[a].at[1 - c],
                                         send_sem=send_sems.at[a], recv_sem=recv_sems.at[a],
                                         device_id=sib, device_id_type=MESH).wait_recv()
        for cp in cps:
            cp.wait_send()
        for lc in locs:
            lc.wait()

    shapes = [_sds((2, r.shape[0], r.shape[1]), F32) for r in rs]
    return _pcall(body, in_specs=[ANY] * n, out_specs=[ANY] * n, out_shape=shapes,
                  scratch_shapes=[pltpu.SemaphoreType.DMA((n,)), pltpu.SemaphoreType.DMA((n,)),
                                  pltpu.SemaphoreType.DMA((n,))], name=name)(*rs)


def _all_reduce_small(name, p):
    _, r, w = p.shape

    def body(in_ref, out_ref, recv, send_sems, recv_sems):
        x, y, c = _me()
        me = 4 * x + 2 * y + c

        def peer(k):
            px, py, pc = (x + (k >> 2)) % 2, (y + ((k >> 1) & 1)) % 2, (c + (k & 1)) % 2
            return (px, py, pc), 4 * px + 2 * py + pc

        cps = []
        for k in range(1, 8):
            dev, idx = peer(k)
            cp = pltpu.make_async_remote_copy(src_ref=in_ref.at[idx], dst_ref=recv.at[k],
                                              send_sem=send_sems.at[0, k], recv_sem=recv_sems.at[0, k],
                                              device_id=dev, device_id_type=MESH)
            cp.start()
            cps.append(cp)
        for cp in cps:
            cp.wait_recv()
        acc = in_ref[me]
        for k in range(1, 8):
            acc = acc + recv[k]
        out_ref[me] = acc
        cps2 = []
        for k in range(1, 8):
            dev, idx = peer(k)
            cp = pltpu.make_async_remote_copy(src_ref=out_ref.at[me], dst_ref=out_ref.at[me],
                                              send_sem=send_sems.at[1, k], recv_sem=recv_sems.at[1, k],
                                              device_id=dev, device_id_type=MESH)
            cp.start()
            cps2.append(cp)
        for k in range(1, 8):
            dev, idx = peer(k)
            pltpu.make_async_remote_copy(src_ref=out_ref.at[me], dst_ref=out_ref.at[idx],
                                         send_sem=send_sems.at[1, k], recv_sem=recv_sems.at[1, k],
                                         device_id=dev, device_id_type=MESH).wait_recv()
        for cp in cps + cps2:
            cp.wait_send()

    vm = pl.BlockSpec(memory_space=pltpu.VMEM)
    return _pcall(body, in_specs=[vm], out_specs=vm, out_shape=_sds(p.shape, F32),
                  scratch_shapes=[pltpu.VMEM((8, r, w), F32), pltpu.SemaphoreType.DMA((2, 8)),
                                  pltpu.SemaphoreType.DMA((2, 8))],
                  compiler_params=pltpu.CompilerParams(vmem_limit_bytes=VMEM_LIMIT), name=name)(p)


def _rglru_gates(xc, h, wa_ref, ba_ref, wx_ref, bx_ref, la_ref, hs):
    xb = xc.astype(BF16)
    r = jax.nn.sigmoid(_dot(xb, wa_ref[h]) + ba_ref[:, hs])
    ig = jax.nn.sigmoid(_dot(xb, wx_ref[h]) + bx_ref[:, hs])
    log_a = (-RG_C * r) * la_ref[:, hs]
    a = jnp.exp(log_a)
    mult = jnp.sqrt(_neg_expm1(2.0 * log_a))
    return r, ig, a, mult


def _conv_taps(cw_ref, cb_ref, xext, t, hs):
    acc = cb_ref[:, hs] + cw_ref[0:1, hs] * xext[pl.ds(5, t), hs]
    for k in range(1, 4):
        acc = acc + cw_ref[k:k + 1, hs] * xext[pl.ds(5 + k, t), hs]
    return acc


def _rglru_fwd(z, conv_w, conv_b, wa_b, ba, wx_b, bx, la, s, dr, t):
    nh = wa_b.shape[0]
    hd = dr // nh
    lw = dr // SCAN_ROWS
    hpr = lw // hd

    def body(xr_ref, gt_ref, cw_ref, cb_ref, wa_ref, ba_ref, wx_ref, bx_ref, la_ref, h_ref, hg_ref,
             xext, a_s, b_s, hcar):
        i = pl.program_id(0)

        @pl.when(i == 0)
        def _():
            xext[0:8, :] = jnp.zeros((8, dr), F32)
            hcar[...] = jnp.zeros((hpr, 8, hd), F32)

        @pl.when(i > 0)
        def _():
            xext[0:8, :] = xext[t:t + 8, :]

        xext[8:t + 8, :] = xr_ref[...]
        for h in range(nh):
            hs = slice(h * hd, (h + 1) * hd)
            xc = _conv_taps(cw_ref, cb_ref, xext, t, hs)
            _, ig, a, mult = _rglru_gates(xc, h, wa_ref, ba_ref, wx_ref, bx_ref, la_ref, hs)
            a_s[h % hpr, pl.ds(h // hpr, t, stride=8), :] = a
            b_s[h % hpr, pl.ds(h // hpr, t, stride=8), :] = mult * (ig * xc)

        def step(tt, hp):
            o = pl.multiple_of(tt * 8, 8)
            hn = a_s[:, pl.ds(o, 8), :] * hp + b_s[:, pl.ds(o, 8), :]
            b_s[:, pl.ds(o, 8), :] = hn
            return hn

        hcar[...] = lax.fori_loop(0, t, step, hcar[...], unroll=8)
        for h in range(nh):
            hs = slice(h * hd, (h + 1) * hd)
            hj = b_s[h % hpr, pl.ds(h // hpr, t, stride=8), :]
            h_ref[:, hs] = hj
            hg_ref[:, hs] = (hj * _gelu(gt_ref[:, hs])).astype(BF16)

    full = lambda arr: pl.BlockSpec(arr.shape, lambda i: (0,) * arr.ndim)
    return _pcall(
        body, grid=(s // t,),
        in_specs=[pl.BlockSpec((t, dr), lambda i: (i, 0)), pl.BlockSpec((t, dr), lambda i: (i, 1)),
                  full(conv_w), full(conv_b), full(wa_b), full(ba), full(wx_b), full(bx), full(la)],
        out_specs=[pl.BlockSpec((t, dr), lambda i: (i, 0))] * 2,
        out_shape=[_sds((s, dr), F32), _sds((s, dr), BF16)],
        scratch_shapes=[pltpu.VMEM((t + 8, dr), F32), pltpu.VMEM((hpr, t * 8, hd), F32),
                        pltpu.VMEM((hpr, t * 8, hd), F32), pltpu.VMEM((hpr, 8, hd), F32)],
        compiler_params=_cparams(1), name="rglru_fwd")(z, z, conv_w, conv_b, wa_b, ba, wx_b, bx, la)


def _rglru_bwd(dhg, z, hsave, conv_w, conv_b, wa_b, ba, wx_b, bx, la, s, dr, din, t):
    nh = wa_b.shape[0]
    hd = dr // nh
    lw = dr // SCAN_ROWS
    hpr = lw // hd
    nch = s // t
    tb = t // 8

    def body(dhg_ref, xr_ref, xp_ref, gt_ref, h_ref, hp_ref, cw_ref, cb_ref, wa_ref, ba_ref, wx_ref, bx_ref, la_ref,
             dz_ref, dcw_ref, dcb_ref, dwa_ref, dwx_ref, dba_ref, dbx_ref, dla_ref,
             xext, hext, r_s, i_s, a_s, g_s, dxe, car):
        i = pl.program_id(0)
        first = (nch - 1 - i) == 0

        @pl.when(i == 0)
        def _():
            for ref in (dcw_ref, dcb_ref, dwa_ref, dwx_ref, dba_ref, dbx_ref, dla_ref, car):
                ref[...] = jnp.zeros(ref.shape, F32)
            dxe[t:t + 8, :] = jnp.zeros((8, dr), F32)

        keep = jnp.where(first, 0.0, 1.0)
        xext[0:8, :] = xp_ref[...] * keep
        hext[0:8, :] = hp_ref[...] * keep
        xext[8:t + 8, :] = xr_ref[...]
        hext[8:t + 8, :] = h_ref[...]
        for h in range(nh):
            hs = slice(h * hd, (h + 1) * hd)
            xc = _conv_taps(cw_ref, cb_ref, xext, t, hs)
            r, ig, a, _ = _rglru_gates(xc, h, wa_ref, ba_ref, wx_ref, bx_ref, la_ref, hs)
            r_s[:, hs] = r
            i_s[:, hs] = ig
            a_s[h % hpr, pl.ds(h // hpr, t, stride=8), :] = a
            gate = gt_ref[:, hs]
            dh_out = dhg_ref[:, hs]
            g_s[h % hpr, pl.ds(h // hpr, t, stride=8), :] = dh_out * _gelu(gate)
            dz_ref[:, dr + h * hd:dr + (h + 1) * hd] = (dh_out * h_ref[:, hs] * _gelu_grad(gate)).astype(BF16)

        def step(k, cr):
            o = pl.multiple_of((t - 1 - k) * 8, 8)
            dh = g_s[:, pl.ds(o, 8), :] + cr
            g_s[:, pl.ds(o, 8), :] = dh
            return a_s[:, pl.ds(o, 8), :] * dh

        car[...] = lax.fori_loop(0, t, step, car[...], unroll=8)

        for h in range(nh):
            hs = slice(h * hd, (h + 1) * hd)
            dh = g_s[h % hpr, pl.ds(h // hpr, t, stride=8), :]
            a = a_s[h % hpr, pl.ds(h // hpr, t, stride=8), :]
            r = r_s[:, hs]
            ig = i_s[:, hs]
            xc = _conv_taps(cw_ref, cb_ref, xext, t, hs)
            la_h = la_ref[:, hs]
            mult = jnp.sqrt(_neg_expm1(2.0 * ((-RG_C * r) * la_h)))
            da = dh * hext[pl.ds(7, t), hs]
            dmult = dh * (ig * xc)
            dlog_a = da * a - dmult * (a * a) / mult
            dr_ = dlog_a * (-RG_C * la_h)
            dla_ref[:, hs] += _rows_part(dlog_a * (-RG_C * r))
            dpr = dr_ * r * (1.0 - r)
            dpi = (dh * mult * xc) * ig * (1.0 - ig)
            dprb, dpib, xcb = dpr.astype(BF16), dpi.astype(BF16), xc.astype(BF16)
            dxc = dh * mult * ig + _dot_nt(dprb, wa_ref[h]) + _dot_nt(dpib, wx_ref[h])
            dwa_ref[h] += _dot_tn(xcb, dprb)
            dwx_ref[h] += _dot_tn(xcb, dpib)
            dba_ref[:, hs] += _rows_part(dpr)
            dbx_ref[:, hs] += _rows_part(dpi)
            dcb_ref[:, hs] += _rows_part(dxc)
            dxe[0:t, hs] = dxc
            for k in range(4):
                dcw_ref[8 * k:8 * k + 8, hs] += _rows_part(dxc * xext[pl.ds(5 + k, t), hs])
        for h in range(nh):
            hs = slice(h * hd, (h + 1) * hd)
            dxr = cw_ref[3:4, hs] * dxe[pl.ds(0, t), hs]
            for k in range(3):
                dxr = dxr + cw_ref[k:k + 1, hs] * dxe[pl.ds(3 - k, t), hs]
            dz_ref[:, hs] = dxr.astype(BF16)
        dxe[t:t + 8, :] = dxe[0:8, :]

    full = lambda arr: pl.BlockSpec(arr.shape, lambda i: (0,) * arr.ndim)
    rev = lambda col: (lambda i: (nch - 1 - i, col))
    prev = lambda i: (jnp.maximum((nch - 1 - i) * tb - 1, 0), 0)
    acc = lambda shape: pl.BlockSpec(shape, lambda i: (0,) * len(shape))
    return _pcall(
        body, grid=(nch,),
        in_specs=[pl.BlockSpec((t, dr), rev(0)), pl.BlockSpec((t, dr), rev(0)), pl.BlockSpec((8, dr), prev),
                  pl.BlockSpec((t, dr), rev(1)), pl.BlockSpec((t, dr), rev(0)), pl.BlockSpec((8, dr), prev),
                  full(conv_w), full(conv_b), full(wa_b), full(ba), full(wx_b), full(bx), full(la)],
        out_specs=[pl.BlockSpec((t, 2 * dr), rev(0)), acc((32, dr)), acc((8, dr)), acc((nh, hd, hd)), acc((nh, hd, hd)),
                   acc((8, dr)), acc((8, dr)), acc((8, dr))],
        out_shape=[_sds((s, din), BF16), _sds((32, dr), F32), _sds((8, dr), F32), _sds((nh, hd, hd), F32),
                   _sds((nh, hd, hd), F32), _sds((8, dr), F32), _sds((8, dr), F32), _sds((8, dr), F32)],
        scratch_shapes=[pltpu.VMEM((t + 8, dr), F32), pltpu.VMEM((t + 8, dr), F32), pltpu.VMEM((t, dr), F32),
                        pltpu.VMEM((t, dr), F32), pltpu.VMEM((hpr, t * 8, hd), F32), pltpu.VMEM((hpr, t * 8, hd), F32),
                        pltpu.VMEM((t + 8, dr), F32), pltpu.VMEM((hpr, 8, hd), F32)],
        compiler_params=_cparams(1), name="rglru_bwd")(dhg, z, z, z, hsave, hsave, conv_w, conv_b, wa_b, ba, wx_b, bx, la)


LANES = 128


def _scan_put(ref, j, t, val):
    for q in range(ref.shape[0]):
        ref[q, pl.ds(j, t, stride=8), :] = val[:, q * LANES:(q + 1) * LANES]


def _scan_get(ref, j, t):
    return jnp.concatenate([ref[q, pl.ds(j, t, stride=8), :] for q in range(ref.shape[0])], axis=1)


def _s5_fwd(z, bre, bim, cre, cim, lbr, lbi, dvec, s, ds_, t, ucol):
    uw = ds_ // SCAN_ROWS
    nq = lbr.shape[0]
    tile = (nq, 8, LANES)

    def body(u_ref, bre_ref, bim_ref, cre_ref, cim_ref, lbr_ref, lbi_ref, d_ref, ys_ref, yg_ref, hre_ref, him_ref,
             car_re, car_im):
        i = pl.program_id(0)

        @pl.when(i == 0)
        def _():
            car_re[...] = jnp.zeros(tile, F32)
            car_im[...] = jnp.zeros(tile, F32)

        for j in range(SCAN_ROWS):
            uj = u_ref[:, j * uw:(j + 1) * uw].astype(BF16)
            _scan_put(hre_ref, j, t, _dot(uj, bre_ref[j]))
            _scan_put(him_ref, j, t, _dot(uj, bim_ref[j]))
        lr = lbr_ref[...]
        li = lbi_ref[...]

        def step(tt, cr):
            hr, hi = cr
            o = pl.multiple_of(tt * 8, 8)
            nr = lr * hr - li * hi + hre_ref[:, pl.ds(o, 8), :]
            ni = lr * hi + li * hr + him_ref[:, pl.ds(o, 8), :]
            hre_ref[:, pl.ds(o, 8), :] = nr
            him_ref[:, pl.ds(o, 8), :] = ni
            return nr, ni

        cr, ci = lax.fori_loop(0, t, step, (car_re[...], car_im[...]), unroll=8)
        car_re[...] = cr
        car_im[...] = ci
        for j in range(SCAN_ROWS):
            js = slice(j * uw, (j + 1) * uw)
            hr = _scan_get(hre_ref, j, t).astype(BF16)
            hi = _scan_get(him_ref, j, t).astype(BF16)
            yv = _dot(hr, cre_ref[j]) - _dot(hi, cim_ref[j]) + d_ref[:, js] * u_ref[:, js]
            ys_ref[:, js] = yv
            yg_ref[:, js] = _gelu(yv).astype(BF16)

    full = lambda arr: pl.BlockSpec(arr.shape, lambda i: (0,) * arr.ndim)
    hblk = pl.BlockSpec((nq, t * 8, LANES), lambda i: (0, i, 0))
    return _pcall(
        body, grid=(s // t,),
        in_specs=[pl.BlockSpec((t, ds_), lambda i: (i, ucol)), full(bre), full(bim), full(cre), full(cim),
                  full(lbr), full(lbi), full(dvec)],
        out_specs=[pl.BlockSpec((t, ds_), lambda i: (i, 0)), pl.BlockSpec((t, ds_), lambda i: (i, 0)), hblk, hblk],
        out_shape=[_sds((s, ds_), F32), _sds((s, ds_), BF16), _sds((nq, s * 8, LANES), F32),
                   _sds((nq, s * 8, LANES), F32)],
        scratch_shapes=[pltpu.VMEM(tile, F32), pltpu.VMEM(tile, F32)],
        compiler_params=_cparams(1), name="s5_fwd")(z, bre, bim, cre, cim, lbr, lbi, dvec)


def _s5_bwd(dys, z, hre, him, bre, bim, cre, cim, lbr, lbi, dvec, dz, s, ds_, t, ucol):
    uw = ds_ // SCAN_ROWS
    nq = lbr.shape[0]
    tile = (nq, 8, LANES)
    nch = s // t

    def body(dy_ref, u_ref, hre_ref, him_ref, hpr_ref, hpi_ref, bre_ref, bim_ref, cre_ref, cim_ref, lbr_ref, lbi_ref,
             d_ref, dzin_ref, dz_ref, dbre_ref, dbim_ref, dcre_ref, dcim_ref, dlr_ref, dli_ref, dd_ref,
             gre, gim, car_re, car_im):
        i = pl.program_id(0)
        first = (nch - 1 - i) == 0

        @pl.when(i == 0)
        def _():
            for ref in (dbre_ref, dbim_ref, dcre_ref, dcim_ref, dlr_ref, dli_ref, dd_ref, car_re, car_im):
                ref[...] = jnp.zeros(ref.shape, F32)

        for j in range(SCAN_ROWS):
            dyj = dy_ref[:, j * uw:(j + 1) * uw].astype(BF16)
            _scan_put(gre, j, t, _dot_nt(dyj, cre_ref[j]))
            _scan_put(gim, j, t, -_dot_nt(dyj, cim_ref[j]))
        lr = lbr_ref[...]
        li = lbi_ref[...]

        def one(o, hm_re, hm_im, cr):
            c_re, c_im, a_lr, a_li = cr
            g_re = gre[:, pl.ds(o, 8), :] + c_re
            g_im = gim[:, pl.ds(o, 8), :] + c_im
            gre[:, pl.ds(o, 8), :] = g_re
            gim[:, pl.ds(o, 8), :] = g_im
            a_lr = a_lr + (g_re * hm_re + g_im * hm_im)
            a_li = a_li + (g_im * hm_re - g_re * hm_im)
            return lr * g_re + li * g_im, lr * g_im - li * g_re, a_lr, a_li

        def step(k, cr):
            o = pl.multiple_of((t - 1 - k) * 8, 8)
            om = pl.multiple_of((t - 2 - k) * 8, 8)
            return one(o, hre_ref[:, pl.ds(om, 8), :], him_ref[:, pl.ds(om, 8), :], cr)

        zero = jnp.zeros(tile, F32)
        cr = lax.fori_loop(0, t - 1, step, (car_re[...], car_im[...], zero, zero), unroll=8)
        keep = jnp.where(first, 0.0, 1.0)
        c_re, c_im, a_lr, a_li = one(0, hpr_ref[...] * keep, hpi_ref[...] * keep, cr)
        car_re[...] = c_re
        car_im[...] = c_im
        dlr_ref[...] += a_lr
        dli_ref[...] += a_li
        for j in range(SCAN_ROWS):
            js = slice(j * uw, (j + 1) * uw)
            g_r = _scan_get(gre, j, t).astype(BF16)
            g_i = _scan_get(gim, j, t).astype(BF16)
            dyj = dy_ref[:, js]
            uj = u_ref[:, js]
            du = _dot_nt(g_r, bre_ref[j]) + _dot_nt(g_i, bim_ref[j]) + d_ref[:, js] * dyj
            dz_ref[:, js] = du.astype(BF16)
            ujb, dyjb = uj.astype(BF16), dyj.astype(BF16)
            dbre_ref[j] += _dot_tn(ujb, g_r)
            dbim_ref[j] += _dot_tn(ujb, g_i)
            h_r = _scan_get(hre_ref, j, t).astype(BF16)
            h_i = _scan_get(him_ref, j, t).astype(BF16)
            dcre_ref[j] += _dot_tn(h_r, dyjb)
            dcim_ref[j] -= _dot_tn(h_i, dyjb)
            dd_ref[:, js] += _rows_part(dyj * uj)

    full = lambda arr: pl.BlockSpec(arr.shape, lambda i: (0,) * arr.ndim)
    acc = lambda shape: pl.BlockSpec(shape, lambda i: (0,) * len(shape))
    rev = lambda col: (lambda i: (nch - 1 - i, col))
    hblk = pl.BlockSpec((nq, t * 8, LANES), lambda i: (0, nch - 1 - i, 0))
    hprev = pl.BlockSpec(tile, lambda i: (0, jnp.maximum((nch - 1 - i) * t - 1, 0), 0))
    outs = _pcall(
        body, grid=(nch,),
        in_specs=[pl.BlockSpec((t, ds_), rev(0)), pl.BlockSpec((t, ds_), rev(ucol)), hblk, hblk, hprev, hprev,
                  full(bre), full(bim), full(cre), full(cim), full(lbr), full(lbi), full(dvec), ANY],
        out_specs=[pl.BlockSpec((t, ds_), rev(ucol)), acc(bre.shape), acc(bim.shape), acc(cre.shape), acc(cim.shape),
                   acc(tile), acc(tile), acc((8, ds_))],
        out_shape=[_sds(dz.shape, BF16), _sds(bre.shape, F32), _sds(bim.shape, F32), _sds(cre.shape, F32),
                   _sds(cim.shape, F32), _sds(tile, F32), _sds(tile, F32), _sds((8, ds_), F32)],
        scratch_shapes=[pltpu.VMEM((nq, t * 8, LANES), F32), pltpu.VMEM((nq, t * 8, LANES), F32),
                        pltpu.VMEM(tile, F32), pltpu.VMEM(tile, F32)],
        input_output_aliases={13: 0},
        compiler_params=_cparams(1), name="s5_bwd")(dys, z, hre, him, hre, him, bre, bim, cre, cim, lbr, lbi, dvec, dz)
    return outs


def _gate_a_bwd(dmix, ya, z, dz, s, d, tn, col0):
    tm = min(s, 512)

    def body(dm_ref, ya_ref, g_ref, dzin_ref, dz_ref):
        dm = dm_ref[...].astype(F32)
        sg = jax.nn.sigmoid(g_ref[...])
        dz_ref[...] = (dm * ya_ref[...] * sg * (1.0 - sg)).astype(BF16)

    blk = pl.BlockSpec((tm, tn), lambda i, j: (i, j))
    zblk = pl.BlockSpec((tm, tn), lambda i, j: (i, col0 + j))
    return _pcall(body, grid=(s // tm, d // tn), in_specs=[blk, blk, zblk, ANY], out_specs=zblk,
                  out_shape=_sds(dz.shape, BF16), input_output_aliases={3: 0},
                  compiler_params=_cparams(2), name="gate_a_bwd")(dmix, ya, z, dz)


def _gate_b_bwd(dmix, p, q, z, dz, s, d, tn, col0):
    tm = min(s, 512)

    def body(dm_ref, p_ref, q_ref, g_ref, dzin_ref, dz_ref, dp_ref, dq_ref):
        dm = dm_ref[...].astype(F32)
        sg = jax.nn.sigmoid(g_ref[...])
        sq = jax.nn.sigmoid(q_ref[...])
        pv = p_ref[...]
        dz_ref[...] = (dm * (pv * sq) * sg * (1.0 - sg)).astype(BF16)
        dyb = dm * sg
        dp_ref[...] = (dyb * sq).astype(BF16)
        dq_ref[...] = (dyb * pv * sq * (1.0 - sq)).astype(BF16)

    blk = pl.BlockSpec((tm, tn), lambda i, j: (i, j))
    zblk = pl.BlockSpec((tm, tn), lambda i, j: (i, col0 + j))
    return _pcall(body, grid=(s // tm, d // tn), in_specs=[blk, blk, blk, zblk, ANY], out_specs=[zblk, blk, blk],
                  out_shape=[_sds(dz.shape, BF16), _sds((s, d), BF16), _sds((s, d), BF16)],
                  input_output_aliases={4: 0}, compiler_params=_cparams(2), name="gate_b_bwd")(dmix, p, q, z, dz)


def _s5_disc(a_re, a_im, log_dt, b_re, b_im):
    dt = jnp.exp(log_dt)[:, None]
    lr = jnp.minimum(a_re, -1e-4)
    li = a_im
    mag = jnp.exp(lr * dt)
    lbr = mag * jnp.cos(li * dt)
    lbi = mag * jnp.sin(li * dt)
    zr, zi = lbr - 1.0, lbi
    den = lr * lr + li * li
    fr = (zr * lr + zi * li) / den
    fi = (zi * lr - zr * li) / den
    bbr = fr[..., None] * b_re - fi[..., None] * b_im
    bbi = fr[..., None] * b_im + fi[..., None] * b_re
    return lbr, lbi, bbr, bbi


def _s5_bmat(bb):
    g, p, h = bb.shape
    gpb = g // SCAN_ROWS
    eye = jnp.eye(gpb, dtype=F32)
    r = bb.reshape(SCAN_ROWS, gpb, p, h).transpose(0, 1, 3, 2)
    m = r[:, :, :, None, :] * eye[None, :, None, :, None]
    return m.reshape(SCAN_ROWS, gpb * h, gpb * p)


def _s5_bmat_diag(m, g, p, h):
    gpb = g // SCAN_ROWS
    eye = jnp.eye(gpb, dtype=F32)
    r = (m.reshape(SCAN_ROWS, gpb, h, gpb, p) * eye[None, :, None, :, None]).sum(axis=3)
    return r.transpose(0, 1, 3, 2).reshape(g, p, h)


def _s5_cmat(cc):
    g, h, p = cc.shape
    gpb = g // SCAN_ROWS
    eye = jnp.eye(gpb, dtype=F32)
    r = cc.reshape(SCAN_ROWS, gpb, h, p).transpose(0, 1, 3, 2)
    m = r[:, :, :, None, :] * eye[None, :, None, :, None]
    return m.reshape(SCAN_ROWS, gpb * p, gpb * h)


def _s5_cmat_diag(m, g, h, p):
    gpb = g // SCAN_ROWS
    eye = jnp.eye(gpb, dtype=F32)
    r = (m.reshape(SCAN_ROWS, gpb, p, gpb, h) * eye[None, :, None, :, None]).sum(axis=3)
    return r.transpose(0, 1, 3, 2).reshape(g, h, p)


PACK_W = 1024
PACK_TILE = 8 * PACK_W


def _pack(arrs, total_rows):
    parts = []
    for a in arrs:
        f = a.reshape(-1).astype(F32)
        pad = (-f.shape[0]) % PACK_TILE
        parts.append(jnp.pad(f, (0, pad)).reshape(-1, PACK_W))
    rows = sum(p.shape[0] for p in parts)
    if total_rows > rows:
        parts.append(jnp.zeros((total_rows - rows, PACK_W), F32))
    return jnp.concatenate(parts, axis=0)


def _unpack(buf, shapes):
    out, row = [], 0
    for shp in shapes:
        n = math.prod(shp)
        rows = -(-n // PACK_TILE) * 8
        out.append(buf[row:row + rows].reshape(-1)[:n].reshape(shp))
        row += rows
    return out


def _pack_rows(shapes):
    rows = sum(-(-math.prod(s) // PACK_TILE) * 8 for s in shapes)
    return -(-rows // 64) * 64


def kernel(x, w_in, conv_w, conv_b, rg_wa, rg_ba, rg_wx, rg_bx, rg_lambda, w_a_out, ssm_a_re, ssm_a_im, ssm_log_dt, ssm_b_re, ssm_b_im, ssm_c_re, ssm_c_im, ssm_d, glu_w, glu_v, w_out, ln1_g, ln1_b, mlp_w_up, mlp_b_up, mlp_w_down, mlp_b_down, ln2_g, ln2_b, loss_target, m_w_in, m_conv_w, m_conv_b, m_rg_wa, m_rg_ba, m_rg_wx, m_rg_bx, m_rg_lambda, m_w_a_out, m_ssm_a_re, m_ssm_a_im, m_ssm_log_dt, m_ssm_b_re, m_ssm_b_im, m_ssm_c_re, m_ssm_c_im, m_ssm_d, m_glu_w, m_glu_v, m_w_out, m_ln1_g, m_ln1_b, m_mlp_w_up, m_mlp_b_up, m_mlp_w_down, m_mlp_b_down, m_ln2_g, m_ln2_b, v_w_in, v_conv_w, v_conv_b, v_rg_wa, v_rg_ba, v_rg_wx, v_rg_bx, v_rg_lambda, v_w_a_out, v_ssm_a_re, v_ssm_a_im, v_ssm_log_dt, v_ssm_b_re, v_ssm_b_im, v_ssm_c_re, v_ssm_c_im, v_ssm_d, v_glu_w, v_glu_v, v_w_out, v_ln1_g, v_ln1_b, v_mlp_w_up, v_mlp_b_up, v_mlp_w_down, v_mlp_b_down, v_ln2_g, v_ln2_b):
    local = dict(locals())
    s, d = x.shape[1], x.shape[2]
    dr, ds_ = d, d // 2
    din4 = w_in.shape[2]
    din = N_CHIPS * din4
    df4 = mlp_w_up.shape[2]
    df = N_CHIPS * df4
    d4 = d // N_CHIPS
    nh, hd = rg_wa.shape[1], rg_wa.shape[2]
    grp, pst, gh = ssm_b_re.shape[1], ssm_b_re.shape[2], ssm_b_re.shape[3]
    tch = min(s, 256)
    tm = min(s, 1024)
    tmr = min(s, 256)
    xi, yi, ci = _me()
    c_arr = ci.reshape(1).astype(jnp.int32)
    chip_arr = (2 * xi + yi).reshape(1).astype(jnp.int32)

    x2 = x[0]
    tgt = loss_target[0]
    row = lambda a: a.reshape(1, -1)

    def gather(name, w):
        r, c = w.shape
        wb = _cast_bf16("cast_" + name, w)
        return _all_gather_chips("ag_" + name, wb.reshape(2, r // 2, c)).reshape(N_CHIPS, r, c)

    w_in_s = gather("w_in", w_in[0])
    w_a_out_f = gather("w_a_out", w_a_out[0]).reshape(dr, d)
    glu_w_s = gather("glu_w", glu_w[0])
    glu_v_s = gather("glu_v", glu_v[0])
    w_out_f = gather("w_out", w_out[0]).reshape(d, d)
    w_up_s = gather("mlp_w_up", mlp_w_up[0])
    w_down_f = gather("mlp_w_down", mlp_w_down[0]).reshape(df, d)

    cw_place = jnp.zeros((4, dr), F32)
    cw_place = lax.dynamic_update_slice(cw_place, conv_w[0] * (ci == 0).astype(F32), (0, (2 * xi + yi) * d4))
    cw_rows = max(8, (4 * dr) // (8 * PACK_W))
    cw_pad = jnp.zeros((8 * cw_rows * PACK_W,), F32).at[:4 * dr].set(cw_place.reshape(-1))
    cw_full = _all_reduce_small("ar_conv_w", cw_pad.reshape(8, cw_rows, PACK_W)).reshape(-1)[:4 * dr].reshape(4, dr)

    lam = rg_lambda[0]
    la = row(jax.nn.softplus(-lam))
    wa_b = rg_wa[0].astype(BF16)
    wx_b = rg_wx[0].astype(BF16)
    disc_in = (ssm_a_re[0], ssm_a_im[0], ssm_log_dt[0], ssm_b_re[0], ssm_b_im[0])
    (lbr, lbi, bbr, bbi), disc_vjp = jax.vjp(_s5_disc, *disc_in)
    bre_m, bim_m = _s5_bmat(bbr).astype(BF16), _s5_bmat(bbi).astype(BF16)
    cre_m, cim_m = _s5_cmat(ssm_c_re[0]).astype(BF16), _s5_cmat(ssm_c_im[0]).astype(BF16)
    to_tile = lambda a: a.reshape(SCAN_ROWS, -1, LANES).transpose(1, 0, 2)
    from_tile = lambda a: a.transpose(1, 0, 2).reshape(grp, pst)
    lbr_t, lbi_t = to_tile(lbr), to_tile(lbi)
    dvec = row(ssm_d[0])

    tn_in = d // 8
    n4 = din4 // tn_in
    z, = _mm("in_proj", x2, w_in_s,
             pl.BlockSpec((tm, d), lambda i, j, k: (i, 0)),
             pl.BlockSpec((None, d, tn_in), lambda i, j, k: (j // n4, 0, j % n4)),
             NN, (s // tm, din // tn_in, 1), [_sds((s, din), F32)],
             [pl.BlockSpec((tm, tn_in), lambda i, j, k: (i, j))], _store())
    ucol = (2 * dr) // ds_
    hsave, hg = _rglru_fwd(z, cw_full, row(conv_b[0]), wa_b, row(rg_ba[0]), wx_b, row(rg_bx[0]), la, s, dr, tch)
    ys, yg, hre, him = _s5_fwd(z, bre_m, bim_m, cre_m, cim_m, lbr_t, lbi_t, dvec, s, ds_, tch, ucol)

    tn4 = d4
    ya, = _mm("ya", hg, w_a_out_f,
              pl.BlockSpec((tm, dr), lambda i, j, k: (i, 0)), pl.BlockSpec((dr, tn4), lambda i, j, k: (0, j)),
              NN, (s // tm, d // tn4, 1), [_sds((s, d), F32)], [pl.BlockSpec((tm, tn4), lambda i, j, k: (i, j))], _store())
    glu_spec = pl.BlockSpec((None, ds_, tn4), lambda i, j, k: (j, 0, 0))
    pp, = _mm("glu_p", yg, glu_w_s, pl.BlockSpec((tm, ds_), lambda i, j, k: (i, 0)), glu_spec,
              NN, (s // tm, N_CHIPS, 1), [_sds((s, d), F32)], [pl.BlockSpec((tm, tn4), lambda i, j, k: (i, j))], _store())
    col_ga = (2 * dr + ds_) // tn4
    col_gb = col_ga + d // tn4

    def mix_epi(acc, ex, outs):
        p_ref, ya_ref, ga_ref, gb_ref = ex
        outs[0][...] = acc
        mix = jax.nn.sigmoid(ga_ref[...]) * ya_ref[...] + jax.nn.sigmoid(gb_ref[...]) * (p_ref[...] * jax.nn.sigmoid(acc))
        outs[1][...] = mix.astype(BF16)

    blk4 = pl.BlockSpec((tm, tn4), lambda i, j, k: (i, j))
    qq, mixb = _mm("glu_q_mix", yg, glu_v_s, pl.BlockSpec((tm, ds_), lambda i, j, k: (i, 0)), glu_spec,
                   NN, (s // tm, N_CHIPS, 1), [_sds((s, d), F32), _sds((s, d), BF16)], [blk4, blk4], mix_epi,
                   extras=(pp, ya, z, z),
                   extra_specs=(blk4, blk4, pl.BlockSpec((tm, tn4), lambda i, j, k: (i, col_ga + j)),
                                pl.BlockSpec((tm, tn4), lambda i, j, k: (i, col_gb + j))))

    g1, b1, g2, b2 = row(ln1_g[0]), row(ln1_b[0]), row(ln2_g[0]), row(ln2_b[0])
    vec = lambda n: pl.BlockSpec((1, n), lambda i, j, k: (0, 0))
    rowblk = pl.BlockSpec((tmr, d), lambda i, j, k: (i, 0))
    colblk1 = pl.BlockSpec((tmr, 1), lambda i, j, k: (i, 0))
    part_blk = pl.BlockSpec((8, d), lambda i, j, k: (i, 0))

    def ln1_epi(acc, ex, outs):
        x_ref, g_ref, b_ref = ex
        r1 = ALPHA * x_ref[...] + acc
        mu = jnp.mean(r1, axis=-1, keepdims=True)
        cen = r1 - mu
        var = jnp.mean(cen * cen, axis=-1, keepdims=True)
        rstd = lax.rsqrt(var + LN_EPS)
        xh = cen * rstd
        outs[0][...] = xh
        outs[1][...] = (xh * g_ref[...] + b_ref[...]).astype(BF16)
        outs[2][...] = rstd

    xhat1, x1b, rstd1 = _mm("out_proj_ln1", mixb, w_out_f, rowblk, pl.BlockSpec((d, d), lambda i, j, k: (0, 0)),
                            NN, (s // tmr, 1, 1), [_sds((s, d), F32), _sds((s, d), BF16), _sds((s, 1), F32)],
                            [rowblk, rowblk, colblk1], ln1_epi, extras=(x2, g1, b1), extra_specs=(rowblk, vec(d), vec(d)))

    tnf = min(df4, 1024)
    nf4 = df4 // tnf

    def up_epi(acc, ex, outs):
        hp = acc + ex[0][...]
        rl = jnp.maximum(hp, 0.0)
        outs[0][...] = (rl * rl).astype(BF16)
        outs[1][...] = rl.astype(BF16)

    fblk = pl.BlockSpec((tm, tnf), lambda i, j, k: (i, j))
    hact, hrelu = _mm("mlp_up", x1b, w_up_s, pl.BlockSpec((tm, d), lambda i, j, k: (i, 0)),
                      pl.BlockSpec((None, d, tnf), lambda i, j, k: (j // nf4, 0, j % nf4)),
                      NN, (s // tm, df // tnf, 1), [_sds((s, df), BF16), _sds((s, df), BF16)], [fblk, fblk], up_epi,
                      extras=(row(mlp_b_up[0]),), extra_specs=(pl.BlockSpec((1, tnf), lambda i, j, k: (0, j)),))

    tkd = min(df, 1024)

    def down_epi(acc, ex, outs):
        xh1_ref, t_ref, g1_ref, b1_ref, g2_ref, b2_ref, bd_ref = ex
        x1 = xh1_ref[...] * g1_ref[...] + b1_ref[...]
        r2 = ALPHA * x1 + (acc + bd_ref[...])
        mu = jnp.mean(r2, axis=-1, keepdims=True)
        cen = r2 - mu
        var = jnp.mean(cen * cen, axis=-1, keepdims=True)
        rstd = lax.rsqrt(var + LN_EPS)
        xh2 = cen * rstd
        err = (xh2 * g2_ref[...] + b2_ref[...]) - t_ref[...]
        tot = 0.5 * jnp.sum(jnp.mean(err * err, axis=-1, keepdims=True))
        rows_i = lax.broadcasted_iota(jnp.int32, (8, 128), 0)
        cols_i = lax.broadcasted_iota(jnp.int32, (8, 128), 1)
        outs[5][...] = jnp.where((rows_i == 0) & (cols_i == 0), tot, 0.0)
        dy = err * (1.0 / d)
        outs[2][...] = _rows_part(dy * xh2)
        outs[3][...] = _rows_part(dy)
        dxh = dy * g2_ref[...]
        m1 = jnp.mean(dxh, axis=-1, keepdims=True)
        m2 = jnp.mean(dxh * xh2, axis=-1, keepdims=True)
        dr2 = rstd * (dxh - m1 - xh2 * m2)
        outs[0][...] = dr2
        outs[1][...] = dr2.astype(BF16)
        outs[4][...] = _rows_part(dr2)

    nrb = s // tmr
    dr2, dr2b, dg2p, db2p, dbdp, lossp = _mm(
        "mlp_down_ln2_loss", hact, w_down_f, pl.BlockSpec((tmr, tkd), lambda i, j, k: (i, k)),
        pl.BlockSpec((tkd, d), lambda i, j, k: (k, 0)), NN, (nrb, 1, df // tkd),
        [_sds((s, d), F32), _sds((s, d), BF16), _sds((nrb * 8, d), F32), _sds((nrb * 8, d), F32),
         _sds((nrb * 8, d), F32), _sds((nrb * 8, 128), F32)],
        [rowblk, rowblk, part_blk, part_blk, part_blk, pl.BlockSpec((8, 128), lambda i, j, k: (i, 0))], down_epi,
        extras=(xhat1, tgt, g1, b1, g2, b2, row(mlp_b_down[0])),
        extra_specs=(rowblk, rowblk, vec(d), vec(d), vec(d), vec(d), vec(d)), acc_shape=(tmr, d))

    def dh_epi(acc, ex, outs):
        dh = acc * (2.0 * ex[0][...].astype(F32))
        outs[0][...] = dh.astype(BF16)
        outs[1][...] = _rows_part(dh)

    ntb = s // tm
    dhpre, dbup_p = _mm("mlp_down_bwd", dr2b, w_down_f, pl.BlockSpec((tm, d), lambda i, j, k: (i, 0)),
                        pl.BlockSpec((tnf, d), lambda i, j, k: (j, 0)), NT, (ntb, df // tnf, 1),
                        [_sds((s, df), BF16), _sds((ntb * 8, df), F32)],
                        [fblk, pl.BlockSpec((8, tnf), lambda i, j, k: (i, j))], dh_epi,
                        extras=(hrelu,), extra_specs=(fblk,))

    tt = min(s, 512)
    ntt = s // tt
    tkk = min(d, 1024)
    both = lambda shape: [_sds(shape, F32), _sds(shape, BF16)]
    sq_blk = pl.BlockSpec((tkk, tkk), lambda i, j, k: (i, j))
    g_down = _mm("grad_w_down", hact, dr2b, pl.BlockSpec((tt, tkk), lambda i, j, k: (k, i)),
                 pl.BlockSpec((tt, tkk), lambda i, j, k: (k, j)), TN, (df // tkk, d // tkk, ntt),
                 both((df, d)), [sq_blk, sq_blk], _store_both, acc_shape=(tkk, tkk))
    up_blk = pl.BlockSpec((None, tkk, tnf), lambda i, j, k: (j // nf4, i, j % nf4))
    g_up = _mm("grad_w_up", x1b, dhpre, pl.BlockSpec((tt, tkk), lambda i, j, k: (k, i)),
               pl.BlockSpec((tt, tnf), lambda i, j, k: (k, j)), TN, (d // tkk, df // tnf, ntt),
               both((N_CHIPS, d, df4)), [up_blk, up_blk], _store_both, acc_shape=(tkk, tnf))

    def ln1_bwd_epi(acc, ex, outs):
        dr2_ref, xh_ref, rs_ref, g_ref = ex
        dx1 = ALPHA * dr2_ref[...] + acc
        xh = xh_ref[...]
        outs[2][...] = _rows_part(dx1 * xh)
        outs[3][...] = _rows_part(dx1)
        dxh = dx1 * g_ref[...]
        m1 = jnp.mean(dxh, axis=-1, keepdims=True)
        m2 = jnp.mean(dxh * xh, axis=-1, keepdims=True)
        dr1 = rs_ref[...] * (dxh - m1 - xh * m2)
        outs[0][...] = dr1
        outs[1][...] = dr1.astype(BF16)

    dr1, dr1b, dg1p, db1p = _mm(
        "mlp_up_bwd_ln1_bwd", dhpre, w_up_s, pl.BlockSpec((tmr, tnf), lambda i, j, k: (i, k)),
        pl.BlockSpec((None, d, tnf), lambda i, j, k: (k // nf4, 0, k % nf4)), NT, (nrb, 1, df // tnf),
        [_sds((s, d), F32), _sds((s, d), BF16), _sds((nrb * 8, d), F32), _sds((nrb * 8, d), F32)],
        [rowblk, rowblk, part_blk, part_blk], ln1_bwd_epi,
        extras=(dr2, xhat1, rstd1, g1), extra_specs=(rowblk, rowblk, colblk1, vec(d)), acc_shape=(tmr, d))

    dmix, = _mm("out_proj_bwd", dr1b, w_out_f, pl.BlockSpec((tm, d), lambda i, j, k: (i, 0)),
                pl.BlockSpec((tn4, d), lambda i, j, k: (j, 0)), NT, (ntb, d // tn4, 1),
                [_sds((s, d), BF16)], [blk4], _store())
    g_out = _mm("grad_w_out", mixb, dr1b, pl.BlockSpec((tt, tkk), lambda i, j, k: (k, i)),
                pl.BlockSpec((tt, tkk), lambda i, j, k: (k, j)), TN, (d // tkk, d // tkk, ntt),
                both((d, d)), [sq_blk, sq_blk], _store_both, acc_shape=(tkk, tkk))

    def dya_body(dm_ref, g_ref, o_ref):
        o_ref[...] = (dm_ref[...].astype(F32) * jax.nn.sigmoid(g_ref[...])).astype(BF16)

    tme = min(s, 512)
    eblk = pl.BlockSpec((tme, tn4), lambda i, j: (i, j))
    dya = _pcall(dya_body, grid=(s // tme, d // tn4),
                 in_specs=[eblk, pl.BlockSpec((tme, tn4), lambda i, j: (i, col_ga + j))], out_specs=eblk,
                 out_shape=_sds((s, d), BF16), compiler_params=_cparams(2), name="dya")(dmix, z)
    g_a_out = _mm("grad_w_a_out", hg, dya, pl.BlockSpec((tt, tkk), lambda i, j, k: (k, i)),
                  pl.BlockSpec((tt, tkk), lambda i, j, k: (k, j)), TN, (dr // tkk, d // tkk, ntt),
                  both((dr, d)), [sq_blk, sq_blk], _store_both, acc_shape=(tkk, tkk))
    dhg, = _mm("a_out_bwd", dya, w_a_out_f, pl.BlockSpec((tm, d), lambda i, j, k: (i, 0)),
               pl.BlockSpec((tn4, d), lambda i, j, k: (j, 0)), NT, (ntb, dr // tn4, 1),
               [_sds((s, dr), F32)], [blk4], _store())
    dz, dcw_p, dcb_p, dwa, dwx, dba_p, dbx_p, dla_p = _rglru_bwd(
        dhg, z, hsave, cw_full, row(conv_b[0]), wa_b, row(rg_ba[0]), wx_b, row(rg_bx[0]), la, s, dr, din, tch)
    dz = _gate_a_bwd(dmix, ya, z, dz, s, d, tn4, col_ga)
    dz, dpb, dqb = _gate_b_bwd(dmix, pp, qq, z, dz, s, d, tn4, col_gb)

    glu_gblk = pl.BlockSpec((None, ds_, tn4), lambda i, j, k: (j, 0, 0))
    g_glu_w = _mm("grad_glu_w", yg, dpb, pl.BlockSpec((tt, ds_), lambda i, j, k: (k, 0)),
                  pl.BlockSpec((tt, tn4), lambda i, j, k: (k, j)), TN, (1, N_CHIPS, ntt),
                  both((N_CHIPS, ds_, d4)), [glu_gblk, glu_gblk], _store_both, acc_shape=(ds_, tn4))
    g_glu_v = _mm("grad_glu_v", yg, dqb, pl.BlockSpec((tt, ds_), lambda i, j, k: (k, 0)),
                  pl.BlockSpec((tt, tn4), lambda i, j, k: (k, j)), TN, (1, N_CHIPS, ntt),
                  both((N_CHIPS, ds_, d4)), [glu_gblk, glu_gblk], _store_both, acc_shape=(ds_, tn4))
    sblk = pl.BlockSpec((tm, ds_), lambda i, j, k: (i, 0))
    glu_bspec = pl.BlockSpec((None, ds_, tn4), lambda i, j, k: (k, 0, 0))
    dyg_p, = _mm("glu_w_bwd", dpb, glu_w_s, pl.BlockSpec((tm, tn4), lambda i, j, k: (i, k)), glu_bspec,
                 NT, (ntb, 1, N_CHIPS), [_sds((s, ds_), F32)], [sblk], _store(), acc_shape=(tm, ds_))

    def dys_epi(acc, ex, outs):
        outs[0][...] = (acc + ex[0][...]) * _gelu_grad(ex[1][...])

    dys, = _mm("glu_v_bwd_gelu_bwd", dqb, glu_v_s, pl.BlockSpec((tm, tn4), lambda i, j, k: (i, k)), glu_bspec,
               NT, (ntb, 1, N_CHIPS), [_sds((s, ds_), F32)], [sblk], dys_epi,
               extras=(dyg_p, ys), extra_specs=(sblk, sblk), acc_shape=(tm, ds_))
    dz, dbre, dbim, dcre, dcim, dlr, dli, ddp = _s5_bwd(dys, z, hre, him, bre_m, bim_m, cre_m, cim_m, lbr_t, lbi_t,
                                                      dvec, dz, s, ds_, tch, ucol)

    tni = din4 // 3
    ni4 = 3
    in_gblk = pl.BlockSpec((None, tkk, tni), lambda i, j, k: (j // ni4, i, j % ni4))
    g_in = _mm("grad_w_in", x2, dz, pl.BlockSpec((tt, tkk), lambda i, j, k: (k, i)),
               pl.BlockSpec((tt, tni), lambda i, j, k: (k, j)), TN, (d // tkk, din // tni, ntt),
               both((N_CHIPS, d, din4)), [in_gblk, in_gblk], _store_both, acc_shape=(tkk, tni))

    def dx_epi(acc, ex, outs):
        outs[0][...] = ALPHA * ex[0][...] + acc

    tmx = min(s, 512)
    xblk = pl.BlockSpec((tmx, d), lambda i, j, k: (i, 0))
    grad_x, = _mm("in_proj_bwd", dz, w_in_s, pl.BlockSpec((tmx, tni), lambda i, j, k: (i, k)),
                  pl.BlockSpec((None, d, tni), lambda i, j, k: (k // ni4, 0, k % ni4)), NT, (s // tmx, 1, din // tni),
                  [_sds((s, d), F32)], [xblk], dx_epi, extras=(dr1,), extra_specs=(xblk,), acc_shape=(tmx, d))

    stack = lambda g, r: g.reshape(N_CHIPS, 2, r // 2, g.shape[-1])
    big = [("w_in", g_in, d), ("w_a_out", g_a_out, dr // N_CHIPS), ("glu_w", g_glu_w, ds_), ("glu_v", g_glu_v, ds_),
           ("w_out", g_out, d4), ("mlp_w_up", g_up, d), ("mlp_w_down", g_down, df4)]
    names_big = [n for n, _, _ in big]
    g32 = [stack(g[0], r) for _, g, r in big]
    g16 = [stack(g[1], r) for _, g, r in big]
    recv1 = _rs_to_sibling("rs_sibling", g16)
    hsum = [_add_half("rs_add_half_" + n, g, r, c_arr) for n, g, r in zip(names_big, g32, recv1)]
    recv2 = _rs_to_chips("rs_chips", [h[1] for h in hsum])
    rsum = [_add_chips("rs_add_chips_" + n, h[0], r, chip_arr) for n, h, r in zip(names_big, hsum, recv2)]
    joined = _rs_join("rs_join", rsum)
    res = {}
    for n, gj in zip(names_big, joined):
        w = local[n][0]
        g = gj.reshape(w.shape)
        dl, nm, nv = _adamw("adamw_" + n, w, g, local["m_" + n][0], local["v_" + n][0])
        res[n] = (g[None], dl[None], nm[None], nv[None])

    s8 = lambda p: p.sum(axis=0)
    red_names = ["loss", "conv_w", "conv_b", "rg_wa", "rg_ba", "rg_wx", "rg_bx", "dla", "lbr", "lbi", "bbr", "bbi",
                 "c_re", "c_im", "ssm_d", "ln1_g", "ln1_b", "mlp_b_up", "mlp_b_down", "ln2_g", "ln2_b"]
    red_vals = [lossp.sum().reshape(1), dcw_p.reshape(4, 8, dr).sum(axis=1), s8(dcb_p), dwa, s8(dba_p), dwx, s8(dbx_p),
                s8(dla_p), from_tile(dlr), from_tile(dli), _s5_bmat_diag(dbre, grp, pst, gh),
                _s5_bmat_diag(dbim, grp, pst, gh), _s5_cmat_diag(dcre, grp, gh, pst), _s5_cmat_diag(dcim, grp, gh, pst),
                s8(ddp).reshape(grp, gh), s8(dg1p), s8(db1p), s8(dbup_p), s8(dbdp), s8(dg2p), s8(db2p)]
    red_shapes = [v.shape for v in red_vals]
    rows_r = _pack_rows(red_shapes)
    red = _all_reduce_small("ar_small", _pack(red_vals, rows_r).reshape(8, rows_r // 8, PACK_W))
    rv = dict(zip(red_names, _unpack(red.reshape(rows_r, PACK_W), red_shapes)))
    loss = rv["loss"][0]
    d_are, d_aim, d_ldt, d_bre, d_bim = disc_vjp((rv["lbr"], rv["lbi"], rv["bbr"], rv["bbi"]))
    small_g = {"conv_w": rv["conv_w"], "conv_b": rv["conv_b"], "rg_wa": rv["rg_wa"], "rg_ba": rv["rg_ba"].reshape(nh, hd),
               "rg_wx": rv["rg_wx"], "rg_bx": rv["rg_bx"].reshape(nh, hd), "rg_lambda": rv["dla"] * (-jax.nn.sigmoid(-lam)),
               "ssm_a_re": d_are, "ssm_a_im": d_aim, "ssm_log_dt": d_ldt, "ssm_b_re": d_bre, "ssm_b_im": d_bim,
               "ssm_c_re": rv["c_re"], "ssm_c_im": rv["c_im"], "ssm_d": rv["ssm_d"], "ln1_g": rv["ln1_g"],
               "ln1_b": rv["ln1_b"], "mlp_b_up": rv["mlp_b_up"], "mlp_b_down": rv["mlp_b_down"],
               "ln2_g": rv["ln2_g"], "ln2_b": rv["ln2_b"]}
    small_names = list(small_g)
    col0 = (2 * xi + yi) * d4

    def placed(n, a):
        if n != "conv_w":
            return a[0]
        return lax.dynamic_update_slice(jnp.zeros((4, dr), F32), a[0], (0, col0))

    sm_shapes = [small_g[n].shape for n in small_names]
    rows_s = _pack_rows(sm_shapes)
    packs = [_pack([small_g[n] for n in small_names], rows_s)]
    for pre in ("", "m_", "v_"):
        packs.append(_pack([placed(n, local[pre + n]) for n in small_names], rows_s))
    dl_p, nm_p, nv_p = _adamw("adamw_small", packs[1], packs[0], packs[2], packs[3])
    dl_s = dict(zip(small_names, _unpack(dl_p, sm_shapes)))
    nm_s = dict(zip(small_names, _unpack(nm_p, sm_shapes)))
    nv_s = dict(zip(small_names, _unpack(nv_p, sm_shapes)))
    for n in small_names:
        tup = (small_g[n], dl_s[n], nm_s[n], nv_s[n])
        if n == "conv_w":
            tup = tuple(lax.dynamic_slice(a, (0, col0), (4, d4)) for a in tup)
        res[n] = tuple(a.reshape(local[n].shape) for a in tup)

    order = ["w_in", "conv_w", "conv_b", "rg_wa", "rg_ba", "rg_wx", "rg_bx", "rg_lambda", "w_a_out", "ssm_a_re",
             "ssm_a_im", "ssm_log_dt", "ssm_b_re", "ssm_b_im", "ssm_c_re", "ssm_c_im", "ssm_d", "glu_w", "glu_v",
             "w_out", "ln1_g", "ln1_b", "mlp_w_up", "mlp_b_up", "mlp_w_down", "mlp_b_down", "ln2_g", "ln2_b"]
    outs = [loss, grad_x[None]]
    for part in range(4):
        outs += [res[n][part] for n in order]
    return tuple(outs)
```

```python
import functools
import math

import jax
import jax.numpy as jnp
from jax import lax
from jax.experimental import pallas as pl
from jax.experimental.pallas import tpu as pltpu

F32 = jnp.float32
BF16 = jnp.bfloat16
MESH = pl.DeviceIdType.MESH
ANY = pl.BlockSpec(memory_space=pl.ANY)

ALPHA = 2.0 ** 0.25
LN_EPS = 1e-5
RG_C = 8.0
ADAM_LR, ADAM_B1, ADAM_B2, ADAM_EPS, ADAM_WD, ADAM_STEP = 0.001, 0.9, 0.999, 1e-08, 0.01, 10
GELU_K0 = math.sqrt(2.0 / math.pi)
GELU_K1 = 0.044715
VMEM_LIMIT = 56 * 1024 * 1024
N_CHIPS = 4
SCAN_ROWS = 8


def _pcall(body, **kw):
    return pl.pallas_call(body, **kw)


def _cparams(n_grid):
    return pltpu.CompilerParams(dimension_semantics=("arbitrary",) * n_grid, vmem_limit_bytes=VMEM_LIMIT)


def _gelu(x):
    return 0.5 * x * (1.0 + jnp.tanh(GELU_K0 * (x + GELU_K1 * x * x * x)))


def _gelu_grad(x):
    t = jnp.tanh(GELU_K0 * (x + GELU_K1 * x * x * x))
    return 0.5 * (1.0 + t) + 0.5 * x * (1.0 - t * t) * GELU_K0 * (1.0 + 3.0 * GELU_K1 * x * x)


def _neg_expm1(x):
    series = x * (1.0 + x * (0.5 + x * (1.0 / 6.0 + x * (1.0 / 24.0))))
    return -jnp.where(x > -0.03, series, jnp.exp(x) - 1.0)


def _rows_part(v):
    m, n = v.shape
    return v.reshape(m // 8, 8, n).sum(axis=0)


def _dot(a, b):
    return jnp.dot(a, b, preferred_element_type=F32)


def _dot_nt(a, b):
    return lax.dot_general(a, b, (((1,), (1,)), ((), ())), preferred_element_type=F32)


def _dot_tn(a, b):
    return lax.dot_general(a, b, (((0,), (0,)), ((), ())), preferred_element_type=F32)


NN = (((1,), (0,)), ((), ()))
NT = (((1,), (1,)), ((), ()))
TN = (((0,), (0,)), ((), ()))


def _mm(name, a, b, a_spec, b_spec, contract, grid, out_shape, out_specs, epilogue,
        extras=(), extra_specs=(), acc_shape=None, aliases=None):
    nk = grid[2]
    ne = len(extras)
    n_out = len(out_shape)

    def body(*refs):
        a_ref, b_ref = refs[0], refs[1]
        ex = refs[2:2 + ne]
        outs = refs[2 + ne:2 + ne + n_out]
        prod = lax.dot_general(a_ref[...].astype(BF16), b_ref[...].astype(BF16), contract,
                               preferred_element_type=F32)
        if nk == 1:
            epilogue(prod, ex, outs)
        else:
            acc = refs[2 + ne + n_out]
            k = pl.program_id(2)

            @pl.when(k == 0)
            def _():
                acc[...] = prod

            @pl.when(k > 0)
            def _():
                acc[...] += prod

            @pl.when(k == nk - 1)
            def _():
                epilogue(acc[...], ex, outs)

    scratch = [pltpu.VMEM(acc_shape, F32)] if nk > 1 else []
    return _pcall(body, grid=grid, in_specs=[a_spec, b_spec, *extra_specs], out_specs=list(out_specs),
                  out_shape=list(out_shape), scratch_shapes=scratch, input_output_aliases=aliases or {},
                  compiler_params=_cparams(3), name=name)(a, b, *extras)


def _store(dtype=None):
    def epi(acc, ex, outs):
        outs[0][...] = acc.astype(outs[0].dtype)
    return epi


def _store_both(acc, ex, outs):
    outs[0][...] = acc
    outs[1][...] = acc.astype(BF16)


def _sds(shape, dtype):
    return jax.ShapeDtypeStruct(tuple(shape), dtype)


def _row_tile(rows, cols, budget=1 << 20):
    t = rows
    while t * cols * 4 > budget and t % 16 == 0:
        t //= 2
    return t


def _cast_bf16(name, w, chip_arr):
    r, c = w.shape
    tr = _row_tile(r, c)

    def body(k_ref, w_ref, o_ref):
        o_ref[...] = w_ref[...].astype(BF16)

    gs = pltpu.PrefetchScalarGridSpec(
        num_scalar_prefetch=1, grid=(r // tr,),
        in_specs=[pl.BlockSpec((tr, c), lambda i, kk: (i, 0))],
        out_specs=pl.BlockSpec((None, tr, c), lambda i, kk: (kk[0], i, 0)))
    return _pcall(body, grid_spec=gs, out_shape=_sds((N_CHIPS, r, c), BF16),
                  compiler_params=_cparams(1), name=name)(chip_arr, w)


def _adamw(name, w, g, m, v):
    r, c = w.shape
    tr = _row_tile(r, c)

    def body(w_ref, g_ref, m_ref, v_ref, d_ref, nm_ref, nv_ref):
        gg = g_ref[...]
        nm = ADAM_B1 * m_ref[...] + (1.0 - ADAM_B1) * gg
        nv = ADAM_B2 * v_ref[...] + (1.0 - ADAM_B2) * (gg * gg)
        m_hat = nm / (1.0 - ADAM_B1 ** ADAM_STEP)
        v_hat = nv / (1.0 - ADAM_B2 ** ADAM_STEP)
        d_ref[...] = -ADAM_LR * (m_hat / (jnp.sqrt(v_hat) + ADAM_EPS) + ADAM_WD * w_ref[...])
        nm_ref[...] = nm
        nv_ref[...] = nv

    spec = pl.BlockSpec((tr, c), lambda i: (i, 0))
    return _pcall(body, grid=(r // tr,), in_specs=[spec] * 4, out_specs=[spec] * 3,
                  out_shape=[_sds((r, c), F32)] * 3, compiler_params=_cparams(1), name=name)(w, g, m, v)


def _add_half(name, g, recv, c_arr):
    _, _, rh, c = g.shape
    tr = _row_tile(rh, c)

    def body(c_ref, g_ref, r_ref, o_ref, ob_ref):
        h = g_ref[...] + r_ref[...].astype(F32)
        o_ref[...] = h
        ob_ref[...] = h.astype(BF16)

    spec3 = pl.BlockSpec((None, tr, c), lambda s, r, cc: (s, r, 0))
    gs = pltpu.PrefetchScalarGridSpec(
        num_scalar_prefetch=1, grid=(N_CHIPS, rh // tr),
        in_specs=[pl.BlockSpec((None, None, tr, c), lambda s, r, cc: (s, cc[0], r, 0)), spec3],
        out_specs=[spec3, spec3])
    return _pcall(body, grid_spec=gs, out_shape=[_sds((N_CHIPS, rh, c), F32), _sds((N_CHIPS, rh, c), BF16)],
                  compiler_params=_cparams(2), name=name)(c_arr, g, recv)


def _add_chips(name, h, recv, sel_arr):
    _, rh, c = h.shape
    tr = _row_tile(rh, c)

    def body(k_ref, h_ref, r0, r1, r2, o_ref):
        o_ref[...] = ((h_ref[...] + r0[...].astype(F32)) + r1[...].astype(F32)) + r2[...].astype(F32)

    def rspec(k):
        return pl.BlockSpec((None, tr, c), lambda r, kk, k=k: (k, r, 0))

    gs = pltpu.PrefetchScalarGridSpec(
        num_scalar_prefetch=1, grid=(rh // tr,),
        in_specs=[pl.BlockSpec((None, tr, c), lambda r, kk: (kk[0], r, 0)), rspec(0), rspec(1), rspec(2)],
        out_specs=pl.BlockSpec((None, tr, c), lambda r, kk: (kk[1], r, 0)))
    return _pcall(body, grid_spec=gs, out_shape=_sds((2, rh, c), F32),
                  compiler_params=_cparams(1), name=name)(sel_arr, h, recv, recv, recv)


def _me():
    return lax.axis_index("x"), lax.axis_index("y"), lax.axis_index("c")


def _peer_chip(x, y, k):
    px = (x + (k >> 1)) % 2
    py = (y + (k & 1)) % 2
    return px, py, 2 * px + py


def _all_gather_chips(name, w2):
    _, _, rh, c_ = w2.shape

    def body(in_ref, out_ref, send_sems, recv_sems):
        x, y, c = _me()
        chip = 2 * x + y
        sib = (x, y, 1 - c)
        started = []
        for k in (1, 2, 3):
            px, py, _ = _peer_chip(x, y, k)
            cp = pltpu.make_async_remote_copy(src_ref=out_ref.at[chip, c], dst_ref=out_ref.at[chip, c],
                                              send_sem=send_sems.at[k - 1], recv_sem=recv_sems.at[k - 1],
                                              device_id=(px, py, c), device_id_type=MESH)
            cp.start()
            started.append(cp)
        for k in (1, 2, 3):
            _, _, pchip = _peer_chip(x, y, k)
            pltpu.make_async_remote_copy(src_ref=out_ref.at[chip, c], dst_ref=out_ref.at[pchip, c],
                                         send_sem=send_sems.at[k - 1], recv_sem=recv_sems.at[k - 1],
                                         device_id=sib, device_id_type=MESH).wait_recv()
            fw = pltpu.make_async_remote_copy(src_ref=out_ref.at[pchip, c], dst_ref=out_ref.at[pchip, c],
                                              send_sem=send_sems.at[2 + k], recv_sem=recv_sems.at[2 + k],
                                              device_id=sib, device_id_type=MESH)
            fw.start()
            started.append(fw)
        for k in (1, 2, 3):
            _, _, pchip = _peer_chip(x, y, k)
            pltpu.make_async_remote_copy(src_ref=out_ref.at[chip, c], dst_ref=out_ref.at[pchip, 1 - c],
                                         send_sem=send_sems.at[2 + k], recv_sem=recv_sems.at[2 + k],
                                         device_id=sib, device_id_type=MESH).wait_recv()
        for cp in started:
            cp.wait_send()

    return _pcall(body, in_specs=[ANY], out_specs=ANY, out_shape=_sds(w2.shape, w2.dtype),
                  scratch_shapes=[pltpu.SemaphoreType.DMA((6,)), pltpu.SemaphoreType.DMA((6,))],
                  input_output_aliases={0: 0}, name=name)(w2)


def _rs_to_sibling(name, gs):
    n = len(gs)

    def body(*refs):
        ins, outs = refs[:n], refs[n:2 * n]
        send_sems, recv_sems = refs[2 * n], refs[2 * n + 1]
        x, y, c = _me()
        sib = (x, y, 1 - c)
        cps = []
        for a in range(n):
            cp = pltpu.make_async_remote_copy(src_ref=ins[a].at[:, 1 - c], dst_ref=outs[a],
                                              send_sem=send_sems.at[a], recv_sem=recv_sems.at[a],
                                              device_id=sib, device_id_type=MESH)
            cp.start()
            cps.append(cp)
        for cp in cps:
            cp.wait_recv()
        for cp in cps:
            cp.wait_send()

    shapes = [_sds((g.shape[0], g.shape[2], g.shape[3]), g.dtype) for g in gs]
    return _pcall(body, in_specs=[ANY] * n, out_specs=[ANY] * n, out_shape=shapes,
                  scratch_shapes=[pltpu.SemaphoreType.DMA((n,)), pltpu.SemaphoreType.DMA((n,))], name=name)(*gs)


def _rs_to_chips(name, hs):
    n = len(hs)

    def body(*refs):
        ins, outs = refs[:n], refs[n:2 * n]
        send_sems, recv_sems = refs[2 * n], refs[2 * n + 1]
        x, y, c = _me()
        cps = []
        for a in range(n):
            for k in (1, 2, 3):
                px, py, pchip = _peer_chip(x, y, k)
                cp = pltpu.make_async_remote_copy(src_ref=ins[a].at[pchip], dst_ref=outs[a].at[k - 1],
                                                  send_sem=send_sems.at[a, k - 1], recv_sem=recv_sems.at[a, k - 1],
                                                  device_id=(px, py, c), device_id_type=MESH)
                cp.start()
                cps.append(cp)
        for cp in cps:
            cp.wait_recv()
        for cp in cps:
            cp.wait_send()

    shapes = [_sds((3, h.shape[1], h.shape[2]), h.dtype) for h in hs]
    return _pcall(body, in_specs=[ANY] * n, out_specs=[ANY] * n, out_shape=shapes,
                  scratch_shapes=[pltpu.SemaphoreType.DMA((n, 3)), pltpu.SemaphoreType.DMA((n, 3))], name=name)(*hs)


def _rs_join(name, rs):
    n = len(rs)

    def body(*refs):
        outs = refs[n:2 * n]
        send_sems, recv_sems = refs[2 * n], refs[2 * n + 1]
        x, y, c = _me()
        sib = (x, y, 1 - c)
        cps = []
        for a in range(n):
            cp = pltpu.make_async_remote_copy(src_ref=outs[a].at[c], dst_ref=outs[a].at[c],
                                              send_sem=send_sems.at[a], recv_sem=recv_sems.at[a],
                                              device_id=sib, device_id_type=MESH)
            cp.start()
            cps.append(cp)
        for a in range(n):
            pltpu.make_async_remote_copy(src_ref=outs[a].at[c], dst_ref=outs[a].at[1 - c],
                                         send_sem=send_sems.at[a], recv_sem=recv_sems.at[a],
                                         device_id=sib, device_id_type=MESH).wait_recv()
        for cp in cps:
            cp.wait_send()

    shapes = [_sds(r.shape, F32) for r in rs]
    return _pcall(body, in_specs=[ANY] * n, out_specs=[ANY] * n, out_shape=shapes,
                  scratch_shapes=[pltpu.SemaphoreType.DMA((n,)), pltpu.SemaphoreType.DMA((n,))],
                  input_output_aliases={a: a for a in range(n)}, name=name)(*rs)


def _all_reduce_small(name, p):
    _, r, w = p.shape

    def body(in_ref, out_ref, recv, send_sems, recv_sems):
        x, y, c = _me()
        me = 4 * x + 2 * y + c

        def peer(k):
            px, py, pc = (x + (k >> 2)) % 2, (y + ((k >> 1) & 1)) % 2, (c + (k & 1)) % 2
            return (px, py, pc), 4 * px + 2 * py + pc

        cps = []
        for k in range(1, 8):
            dev, idx = peer(k)
            cp = pltpu.make_async_remote_copy(src_ref=in_ref.at[idx], dst_ref=recv.at[k],
                                              send_sem=send_sems.at[0, k], recv_sem=recv_sems.at[0, k],
                                              device_id=dev, device_id_type=MESH)
            cp.start()
            cps.append(cp)
        for cp in cps:
            cp.wait_recv()
        acc = in_ref[me]
        for k in range(1, 8):
            acc = acc + recv[k]
        out_ref[me] = acc
        cps2 = []
        for k in range(1, 8):
            dev, idx = peer(k)
            cp = pltpu.make_async_remote_copy(src_ref=out_ref.at[me], dst_ref=out_ref.at[me],
                                              send_sem=send_sems.at[1, k], recv_sem=recv_sems.at[1, k],
                                              device_id=dev, device_id_type=MESH)
            cp.start()
            cps2.append(cp)
        for k in range(1, 8):
            dev, idx = peer(k)
            pltpu.make_async_remote_copy(src_ref=out_ref.at[me], dst_ref=out_ref.at[idx],
                                         send_sem=send_sems.at[1, k], recv_sem=recv_sems.at[1, k],
                                         device_id=dev, device_id_type=MESH).wait_recv()
        for cp in cps + cps2:
            cp.wait_send()

    vm = pl.BlockSpec(memory_space=pltpu.VMEM)
    return _pcall(body, in_specs=[vm], out_specs=vm, out_shape=_sds(p.shape, F32),
                  scratch_shapes=[pltpu.VMEM((8, r, w), F32), pltpu.SemaphoreType.DMA((2, 8)),
                                  pltpu.SemaphoreType.DMA((2, 8))],
                  compiler_params=pltpu.CompilerParams(vmem_limit_bytes=VMEM_LIMIT), name=name)(p)


def _rglru_gates(xc, h, wa_ref, ba_ref, wx_ref, bx_ref, la_ref, hs):
    xb = xc.astype(BF16)
    r = jax.nn.sigmoid(_dot(xb, wa_ref[h]) + ba_ref[:, hs])
    ig = jax.nn.sigmoid(_dot(xb, wx_ref[h]) + bx_ref[:, hs])
    log_a = (-RG_C * r) * la_ref[:, hs]
    a = jnp.exp(log_a)
    mult = jnp.sqrt(_neg_expm1(2.0 * log_a))
    return r, ig, a, mult


def _conv_taps(cw_ref, cb_ref, xext, t, hs):
    acc = cb_ref[:, hs] + cw_ref[0:1, hs] * xext[pl.ds(5, t), hs]
    for k in range(1, 4):
        acc = acc + cw_ref[k:k + 1, hs] * xext[pl.ds(5 + k, t), hs]
    return acc


def _rglru_fwd(z, conv_w, conv_b, wa_b, ba, wx_b, bx, la, s, dr, t):
    nh = wa_b.shape[0]
    hd = dr // nh
    lw = dr // SCAN_ROWS
    hpr = lw // hd

    def body(xr_ref, gt_ref, cw_ref, cb_ref, wa_ref, ba_ref, wx_ref, bx_ref, la_ref, h_ref, hg_ref,
             xext, a_s, b_s, hcar):
        i = pl.program_id(0)

        @pl.when(i == 0)
        def _():
            xext[0:8, :] = jnp.zeros((8, dr), F32)
            hcar[...] = jnp.zeros((hpr, 8, hd), F32)

        @pl.when(i > 0)
        def _():
            xext[0:8, :] = xext[t:t + 8, :]

        xext[8:t + 8, :] = xr_ref[...]
        for h in range(nh):
            hs = slice(h * hd, (h + 1) * hd)
            xc = _conv_taps(cw_ref, cb_ref, xext, t, hs)
            _, ig, a, mult = _rglru_gates(xc, h, wa_ref, ba_ref, wx_ref, bx_ref, la_ref, hs)
            a_s[h % hpr, pl.ds(h // hpr, t, stride=8), :] = a
            b_s[h % hpr, pl.ds(h // hpr, t, stride=8), :] = mult * (ig * xc)

        def step(tt, hp):
            o = pl.multiple_of(tt * 8, 8)
            hn = a_s[:, pl.ds(o, 8), :] * hp + b_s[:, pl.ds(o, 8), :]
            b_s[:, pl.ds(o, 8), :] = hn
            return hn

        hcar[...] = lax.fori_loop(0, t, step, hcar[...], unroll=8)
        for h in range(nh):
            hs = slice(h * hd, (h + 1) * hd)
            hj = b_s[h % hpr, pl.ds(h // hpr, t, stride=8), :]
            h_ref[:, hs] = hj
            hg_ref[:, hs] = (hj * _gelu(gt_ref[:, hs])).astype(BF16)

    full = lambda arr: pl.BlockSpec(arr.shape, lambda i: (0,) * arr.ndim)
    return _pcall(
        body, grid=(s // t,),
        in_specs=[pl.BlockSpec((t, dr), lambda i: (i, 0)), pl.BlockSpec((t, dr), lambda i: (i, 1)),
                  full(conv_w), full(conv_b), full(wa_b), full(ba), full(wx_b), full(bx), full(la)],
        out_specs=[pl.BlockSpec((t, dr), lambda i: (i, 0))] * 2,
        out_shape=[_sds((s, dr), F32), _sds((s, dr), BF16)],
        scratch_shapes=[pltpu.VMEM((t + 8, dr), F32), pltpu.VMEM((hpr, t * 8, hd), F32),
                        pltpu.VMEM((hpr, t * 8, hd), F32), pltpu.VMEM((hpr, 8, hd), F32)],
        compiler_params=_cparams(1), name="rglru_fwd")(z, z, conv_w, conv_b, wa_b, ba, wx_b, bx, la)


def _rglru_bwd(dhg, z, hsave, conv_w, conv_b, wa_b, ba, wx_b, bx, la, s, dr, din, t):
    nh = wa_b.shape[0]
    hd = dr // nh
    lw = dr // SCAN_ROWS
    hpr = lw // hd
    nch = s // t
    tb = t // 8

    def body(dhg_ref, xr_ref, xp_ref, gt_ref, h_ref, hp_ref, cw_ref, cb_ref, wa_ref, ba_ref, wx_ref, bx_ref, la_ref,
             dz_ref, dcw_ref, dcb_ref, dwa_ref, dwx_ref, dba_ref, dbx_ref, dla_ref,
             xext, hext, r_s, i_s, a_s, g_s, dxe, car):
        i = pl.program_id(0)
        first = (nch - 1 - i) == 0

        @pl.when(i == 0)
        def _():
            for ref in (dcw_ref, dcb_ref, dwa_ref, dwx_ref, dba_ref, dbx_ref, dla_ref, car):
                ref[...] = jnp.zeros(ref.shape, F32)
            dxe[t:t + 8, :] = jnp.zeros((8, dr), F32)

        keep = jnp.where(first, 0.0, 1.0)
        xext[0:8, :] = xp_ref[...] * keep
        hext[0:8, :] = hp_ref[...] * keep
        xext[8:t + 8, :] = xr_ref[...]
        hext[8:t + 8, :] = h_ref[...]
        for h in range(nh):
            hs = slice(h * hd, (h + 1) * hd)
            xc = _conv_taps(cw_ref, cb_ref, xext, t, hs)
            r, ig, a, _ = _rglru_gates(xc, h, wa_ref, ba_ref, wx_ref, bx_ref, la_ref, hs)
            r_s[:, hs] = r
            i_s[:, hs] = ig
            a_s[h % hpr, pl.ds(h // hpr, t, stride=8), :] = a
            gate = gt_ref[:, hs]
            dh_out = dhg_ref[:, hs]
            g_s[h % hpr, pl.ds(h // hpr, t, stride=8), :] = dh_out * _gelu(gate)
            dz_ref[:, dr + h * hd:dr + (h + 1) * hd] = (dh_out * h_ref[:, hs] * _gelu_grad(gate)).astype(BF16)

        def step(k, cr):
            o = pl.multiple_of((t - 1 - k) * 8, 8)
            dh = g_s[:, pl.ds(o, 8), :] + cr
            g_s[:, pl.ds(o, 8), :] = dh
            return a_s[:, pl.ds(o, 8), :] * dh

        car[...] = lax.fori_loop(0, t, step, car[...], unroll=8)

        for h in range(nh):
            hs = slice(h * hd, (h + 1) * hd)
            dh = g_s[h % hpr, pl.ds(h // hpr, t, stride=8), :]
            a = a_s[h % hpr, pl.ds(h // hpr, t, stride=8), :]
            r = r_s[:, hs]
            ig = i_s[:, hs]
            xc = _conv_taps(cw_ref, cb_ref, xext, t, hs)
            la_h = la_ref[:, hs]
            mult = jnp.sqrt(_neg_expm1(2.0 * ((-RG_C * r) * la_h)))
            da = dh * hext[pl.ds(7, t), hs]
            dmult = dh * (ig * xc)
            dlog_a = da * a - dmult * (a * a) / mult
            dr_ = dlog_a * (-RG_C * la_h)
            dla_ref[:, hs] += _rows_part(dlog_a * (-RG_C * r))
            dpr = dr_ * r * (1.0 - r)
            dpi = (dh * mult * xc) * ig * (1.0 - ig)
            dprb, dpib, xcb = dpr.astype(BF16), dpi.astype(BF16), xc.astype(BF16)
            dxc = dh * mult * ig + _dot_nt(dprb, wa_ref[h]) + _dot_nt(dpib, wx_ref[h])
            dwa_ref[h] += _dot_tn(xcb, dprb)
            dwx_ref[h] += _dot_tn(xcb, dpib)
            dba_ref[:, hs] += _rows_part(dpr)
            dbx_ref[:, hs] += _rows_part(dpi)
            dcb_ref[:, hs] += _rows_part(dxc)
            dxe[0:t, hs] = dxc
            for k in range(4):
                dcw_ref[8 * k:8 * k + 8, hs] += _rows_part(dxc * xext[pl.ds(5 + k, t), hs])
        for h in range(nh):
            hs = slice(h * hd, (h + 1) * hd)
            dxr = cw_ref[3:4, hs] * dxe[pl.ds(0, t), hs]
            for k in range(3):
                dxr = dxr + cw_ref[k:k + 1, hs] * dxe[pl.ds(3 - k, t), hs]
            dz_ref[:, hs] = dxr.astype(BF16)
        dxe[t:t + 8, :] = dxe[0:8, :]

    full = lambda arr: pl.BlockSpec(arr.shape, lambda i: (0,) * arr.ndim)
    rev = lambda col: (lambda i: (nch - 1 - i, col))
    prev = lambda i: (jnp.maximum((nch - 1 - i) * tb - 1, 0), 0)
    acc = lambda shape: pl.BlockSpec(shape, lambda i: (0,) * len(shape))
    return _pcall(
        body, grid=(nch,),
        in_specs=[pl.BlockSpec((t, dr), rev(0)), pl.BlockSpec((t, dr), rev(0)), pl.BlockSpec((8, dr), prev),
                  pl.BlockSpec((t, dr), rev(1)), pl.BlockSpec((t, dr), rev(0)), pl.BlockSpec((8, dr), prev),
                  full(conv_w), full(conv_b), full(wa_b), full(ba), full(wx_b), full(bx), full(la)],
        out_specs=[pl.BlockSpec((t, 2 * dr), rev(0)), acc((32, dr)), acc((8, dr)), acc((nh, hd, hd)), acc((nh, hd, hd)),
                   acc((8, dr)), acc((8, dr)), acc((8, dr))],
        out_shape=[_sds((s, din), BF16), _sds((32, dr), F32), _sds((8, dr), F32), _sds((nh, hd, hd), F32),
                   _sds((nh, hd, hd), F32), _sds((8, dr), F32), _sds((8, dr), F32), _sds((8, dr), F32)],
        scratch_shapes=[pltpu.VMEM((t + 8, dr), F32), pltpu.VMEM((t + 8, dr), F32), pltpu.VMEM((t, dr), F32),
                        pltpu.VMEM((t, dr), F32), pltpu.VMEM((hpr, t * 8, hd), F32), pltpu.VMEM((hpr, t * 8, hd), F32),
                        pltpu.VMEM((t + 8, dr), F32), pltpu.VMEM((hpr, 8, hd), F32)],
        compiler_params=_cparams(1), name="rglru_bwd")(dhg, z, z, z, hsave, hsave, conv_w, conv_b, wa_b, ba, wx_b, bx, la)


LANES = 128


def _scan_put(ref, j, t, val):
    for q in range(ref.shape[0]):
        ref[q, pl.ds(j, t, stride=8), :] = val[:, q * LANES:(q + 1) * LANES]


def _scan_get(ref, j, t):
    return jnp.concatenate([ref[q, pl.ds(j, t, stride=8), :] for q in range(ref.shape[0])], axis=1)


def _s5_fwd(z, bre, bim, cre, cim, lbr, lbi, dvec, s, ds_, t, ucol):
    uw = ds_ // SCAN_ROWS
    nq = lbr.shape[0]
    tile = (nq, 8, LANES)

    def body(u_ref, bre_ref, bim_ref, cre_ref, cim_ref, lbr_ref, lbi_ref, d_ref, ys_ref, yg_ref, hre_ref, him_ref,
             car_re, car_im):
        i = pl.program_id(0)

        @pl.when(i == 0)
        def _():
            car_re[...] = jnp.zeros(tile, F32)
            car_im[...] = jnp.zeros(tile, F32)

        for j in range(SCAN_ROWS):
            uj = u_ref[:, j * uw:(j + 1) * uw].astype(BF16)
            _scan_put(hre_ref, j, t, _dot(uj, bre_ref[j]))
            _scan_put(him_ref, j, t, _dot(uj, bim_ref[j]))
        lr = lbr_ref[...]
        li = lbi_ref[...]

        def step(tt, cr):
            hr, hi = cr
            o = pl.multiple_of(tt * 8, 8)
            nr = lr * hr - li * hi + hre_ref[:, pl.ds(o, 8), :]
            ni = lr * hi + li * hr + him_ref[:, pl.ds(o, 8), :]
            hre_ref[:, pl.ds(o, 8), :] = nr
            him_ref[:, pl.ds(o, 8), :] = ni
            return nr, ni

        cr, ci = lax.fori_loop(0, t, step, (car_re[...], car_im[...]), unroll=8)
        car_re[...] = cr
        car_im[...] = ci
        for j in range(SCAN_ROWS):
            js = slice(j * uw, (j + 1) * uw)
            hr = _scan_get(hre_ref, j, t).astype(BF16)
            hi = _scan_get(him_ref, j, t).astype(BF16)
            yv = _dot(hr, cre_ref[j]) - _dot(hi, cim_ref[j]) + d_ref[:, js] * u_ref[:, js]
            ys_ref[:, js] = yv
            yg_ref[:, js] = _gelu(yv).astype(BF16)

    full = lambda arr: pl.BlockSpec(arr.shape, lambda i: (0,) * arr.ndim)
    hblk = pl.BlockSpec((nq, t * 8, LANES), lambda i: (0, i, 0))
    return _pcall(
        body, grid=(s // t,),
        in_specs=[pl.BlockSpec((t, ds_), lambda i: (i, ucol)), full(bre), full(bim), full(cre), full(cim),
                  full(lbr), full(lbi), full(dvec)],
        out_specs=[pl.BlockSpec((t, ds_), lambda i: (i, 0)), pl.BlockSpec((t, ds_), lambda i: (i, 0)), hblk, hblk],
        out_shape=[_sds((s, ds_), F32), _sds((s, ds_), BF16), _sds((nq, s * 8, LANES), F32),
                   _sds((nq, s * 8, LANES), F32)],
        scratch_shapes=[pltpu.VMEM(tile, F32), pltpu.VMEM(tile, F32)],
        compiler_params=_cparams(1), name="s5_fwd")(z, bre, bim, cre, cim, lbr, lbi, dvec)


def _s5_bwd(dys, z, hre, him, bre, bim, cre, cim, lbr, lbi, dvec, dz, s, ds_, t, ucol):
    uw = ds_ // SCAN_ROWS
    nq = lbr.shape[0]
    tile = (nq, 8, LANES)
    nch = s // t

    def body(dy_ref, u_ref, hre_ref, him_ref, hpr_ref, hpi_ref, bre_ref, bim_ref, cre_ref, cim_ref, lbr_ref, lbi_ref,
             d_ref, dzin_ref, dz_ref, dbre_ref, dbim_ref, dcre_ref, dcim_ref, dlr_ref, dli_ref, dd_ref,
             gre, gim, car_re, car_im):
        i = pl.program_id(0)
        first = (nch - 1 - i) == 0

        @pl.when(i == 0)
        def _():
            for ref in (dbre_ref, dbim_ref, dcre_ref, dcim_ref, dlr_ref, dli_ref, dd_ref, car_re, car_im):
                ref[...] = jnp.zeros(ref.shape, F32)

        for j in range(SCAN_ROWS):
            dyj = dy_ref[:, j * uw:(j + 1) * uw].astype(BF16)
            _scan_put(gre, j, t, _dot_nt(dyj, cre_ref[j]))
            _scan_put(gim, j, t, -_dot_nt(dyj, cim_ref[j]))
        lr = lbr_ref[...]
        li = lbi_ref[...]

        def one(o, hm_re, hm_im, cr):
            c_re, c_im, a_lr, a_li = cr
            g_re = gre[:, pl.ds(o, 8), :] + c_re
            g_im = gim[:, pl.ds(o, 8), :] + c_im
            gre[:, pl.ds(o, 8), :] = g_re
            gim[:, pl.ds(o, 8), :] = g_im
            a_lr = a_lr + (g_re * hm_re + g_im * hm_im)
            a_li = a_li + (g_im * hm_re - g_re * hm_im)
            return lr * g_re + li * g_im, lr * g_im - li * g_re, a_lr, a_li

        def step(k, cr):
            o = pl.multiple_of((t - 1 - k) * 8, 8)
            om = pl.multiple_of((t - 2 - k) * 8, 8)
            return one(o, hre_ref[:, pl.ds(om, 8), :], him_ref[:, pl.ds(om, 8), :], cr)

        zero = jnp.zeros(tile, F32)
        cr = lax.fori_loop(0, t - 1, step, (car_re[...], car_im[...], zero, zero), unroll=8)
        keep = jnp.where(first, 0.0, 1.0)
        c_re, c_im, a_lr, a_li = one(0, hpr_ref[...] * keep, hpi_ref[...] * keep, cr)
        car_re[...] = c_re
        car_im[...] = c_im
        dlr_ref[...] += a_lr
        dli_ref[...] += a_li
        for j in range(SCAN_ROWS):
            js = slice(j * uw, (j + 1) * uw)
            g_r = _scan_get(gre, j, t).astype(BF16)
            g_i = _scan_get(gim, j, t).astype(BF16)
            dyj = dy_ref[:, js]
            uj = u_ref[:, js]
            du = _dot_nt(g_r, bre_ref[j]) + _dot_nt(g_i, bim_ref[j]) + d_ref[:, js] * dyj
            dz_ref[:, js] = du.astype(BF16)
            ujb, dyjb = uj.astype(BF16), dyj.astype(BF16)
            dbre_ref[j] += _dot_tn(ujb, g_r)
            dbim_ref[j] += _dot_tn(ujb, g_i)
            h_r = _scan_get(hre_ref, j, t).astype(BF16)
            h_i = _scan_get(him_ref, j, t).astype(BF16)
            dcre_ref[j] += _dot_tn(h_r, dyjb)
            dcim_ref[j] -= _dot_tn(h_i, dyjb)
            dd_ref[:, js] += _rows_part(dyj * uj)

    full = lambda arr: pl.BlockSpec(arr.shape, lambda i: (0,) * arr.ndim)
    acc = lambda shape: pl.BlockSpec(shape, lambda i: (0,) * len(shape))
    rev = lambda col: (lambda i: (nch - 1 - i, col))
    hblk = pl.BlockSpec((nq, t * 8, LANES), lambda i: (0, nch - 1 - i, 0))
    hprev = pl.BlockSpec(tile, lambda i: (0, jnp.maximum((nch - 1 - i) * t - 1, 0), 0))
    outs = _pcall(
        body, grid=(nch,),
        in_specs=[pl.BlockSpec((t, ds_), rev(0)), pl.BlockSpec((t, ds_), rev(ucol)), hblk, hblk, hprev, hprev,
                  full(bre), full(bim), full(cre), full(cim), full(lbr), full(lbi), full(dvec), ANY],
        out_specs=[pl.BlockSpec((t, ds_), rev(ucol)), acc(bre.shape), acc(bim.shape), acc(cre.shape), acc(cim.shape),
                   acc(tile), acc(tile), acc((8, ds_))],
        out_shape=[_sds(dz.shape, BF16), _sds(bre.shape, F32), _sds(bim.shape, F32), _sds(cre.shape, F32),
                   _sds(cim.shape, F32), _sds(tile, F32), _sds(tile, F32), _sds((8, ds_), F32)],
        scratch_shapes=[pltpu.VMEM((nq, t * 8, LANES), F32), pltpu.VMEM((nq, t * 8, LANES), F32),
                        pltpu.VMEM(tile, F32), pltpu.VMEM(tile, F32)],
        input_output_aliases={13: 0},
        compiler_params=_cparams(1), name="s5_bwd")(dys, z, hre, him, hre, him, bre, bim, cre, cim, lbr, lbi, dvec, dz)
    return outs


def _gate_a_bwd(dmix, ya, z, dz, s, d, tn, col0):
    tm = min(s, 512)

    def body(dm_ref, ya_ref, g_ref, dzin_ref, dz_ref):
        dm = dm_ref[...].astype(F32)
        sg = jax.nn.sigmoid(g_ref[...])
        dz_ref[...] = (dm * ya_ref[...] * sg * (1.0 - sg)).astype(BF16)

    blk = pl.BlockSpec((tm, tn), lambda i, j: (i, j))
    zblk = pl.BlockSpec((tm, tn), lambda i, j: (i, col0 + j))
    return _pcall(body, grid=(s // tm, d // tn), in_specs=[blk, blk, zblk, ANY], out_specs=zblk,
                  out_shape=_sds(dz.shape, BF16), input_output_aliases={3: 0},
                  compiler_params=_cparams(2), name="gate_a_bwd")(dmix, ya, z, dz)


def _gate_b_bwd(dmix, p, q, z, dz, s, d, tn, col0):
    tm = min(s, 512)

    def body(dm_ref, p_ref, q_ref, g_ref, dzin_ref, dz_ref, dp_ref, dq_ref):
        dm = dm_ref[...].astype(F32)
        sg = jax.nn.sigmoid(g_ref[...])
        sq = jax.nn.sigmoid(q_ref[...])
        pv = p_ref[...]
        dz_ref[...] = (dm * (pv * sq) * sg * (1.0 - sg)).astype(BF16)
        dyb = dm * sg
        dp_ref[...] = (dyb * sq).astype(BF16)
        dq_ref[...] = (dyb * pv * sq * (1.0 - sq)).astype(BF16)

    blk = pl.BlockSpec((tm, tn), lambda i, j: (i, j))
    zblk = pl.BlockSpec((tm, tn), lambda i, j: (i, col0 + j))
    return _pcall(body, grid=(s // tm, d // tn), in_specs=[blk, blk, blk, zblk, ANY], out_specs=[zblk, blk, blk],
                  out_shape=[_sds(dz.shape, BF16), _sds((s, d), BF16), _sds((s, d), BF16)],
                  input_output_aliases={4: 0}, compiler_params=_cparams(2), name="gate_b_bwd")(dmix, p, q, z, dz)


def _s5_disc(a_re, a_im, log_dt, b_re, b_im):
    dt = jnp.exp(log_dt)[:, None]
    lr = jnp.minimum(a_re, -1e-4)
    li = a_im
    mag = jnp.exp(lr * dt)
    lbr = mag * jnp.cos(li * dt)
    lbi = mag * jnp.sin(li * dt)
    zr, zi = lbr - 1.0, lbi
    den = lr * lr + li * li
    fr = (zr * lr + zi * li) / den
    fi = (zi * lr - zr * li) / den
    bbr = fr[..., None] * b_re - fi[..., None] * b_im
    bbi = fr[..., None] * b_im + fi[..., None] * b_re
    return lbr, lbi, bbr, bbi


def _s5_bmat(bb):
    g, p, h = bb.shape
    gpb = g // SCAN_ROWS
    eye = jnp.eye(gpb, dtype=F32)
    r = bb.reshape(SCAN_ROWS, gpb, p, h).transpose(0, 1, 3, 2)
    m = r[:, :, :, None, :] * eye[None, :, None, :, None]
    return m.reshape(SCAN_ROWS, gpb * h, gpb * p)


def _s5_bmat_diag(m, g, p, h):
    gpb = g // SCAN_ROWS
    eye = jnp.eye(gpb, dtype=F32)
    r = (m.reshape(SCAN_ROWS, gpb, h, gpb, p) * eye[None, :, None, :, None]).sum(axis=3)
    return r.transpose(0, 1, 3, 2).reshape(g, p, h)


def _s5_cmat(cc):
    g, h, p = cc.shape
    gpb = g // SCAN_ROWS
    eye = jnp.eye(gpb, dtype=F32)
    r = cc.reshape(SCAN_ROWS, gpb, h, p).transpose(0, 1, 3, 2)
    m = r[:, :, :, None, :] * eye[None, :, None, :, None]
    return m.reshape(SCAN_ROWS, gpb * p, gpb * h)


def _s5_cmat_diag(m, g, h, p):
    gpb = g // SCAN_ROWS
    eye = jnp.eye(gpb, dtype=F32)
    r = (m.reshape(SCAN_ROWS, gpb, p, gpb, h) * eye[None, :, None, :, None]).sum(axis=3)
    return r.transpose(0, 1, 3, 2).reshape(g, h, p)


PACK_W = 1024
PACK_TILE = 8 * PACK_W


def _pack(arrs, total_rows):
    parts = []
    for a in arrs:
        f = a.reshape(-1).astype(F32)
        pad = (-f.shape[0]) % PACK_TILE
        parts.append(jnp.pad(f, (0, pad)).reshape(-1, PACK_W))
    rows = sum(p.shape[0] for p in parts)
    if total_rows > rows:
        parts.append(jnp.zeros((total_rows - rows, PACK_W), F32))
    return jnp.concatenate(parts, axis=0)


def _unpack(buf, shapes):
    out, row = [], 0
    for shp in shapes:
        n = math.prod(shp)
        rows = -(-n // PACK_TILE) * 8
        out.append(buf[row:row + rows].reshape(-1)[:n].reshape(shp))
        row += rows
    return out


def _pack_rows(shapes):
    rows = sum(-(-math.prod(s) // PACK_TILE) * 8 for s in shapes)
    return -(-rows // 64) * 64


def kernel(x, w_in, conv_w, conv_b, rg_wa, rg_ba, rg_wx, rg_bx, rg_lambda, w_a_out, ssm_a_re, ssm_a_im, ssm_log_dt, ssm_b_re, ssm_b_im, ssm_c_re, ssm_c_im, ssm_d, glu_w, glu_v, w_out, ln1_g, ln1_b, mlp_w_up, mlp_b_up, mlp_w_down, mlp_b_down, ln2_g, ln2_b, loss_target, m_w_in, m_conv_w, m_conv_b, m_rg_wa, m_rg_ba, m_rg_wx, m_rg_bx, m_rg_lambda, m_w_a_out, m_ssm_a_re, m_ssm_a_im, m_ssm_log_dt, m_ssm_b_re, m_ssm_b_im, m_ssm_c_re, m_ssm_c_im, m_ssm_d, m_glu_w, m_glu_v, m_w_out, m_ln1_g, m_ln1_b, m_mlp_w_up, m_mlp_b_up, m_mlp_w_down, m_mlp_b_down, m_ln2_g, m_ln2_b, v_w_in, v_conv_w, v_conv_b, v_rg_wa, v_rg_ba, v_rg_wx, v_rg_bx, v_rg_lambda, v_w_a_out, v_ssm_a_re, v_ssm_a_im, v_ssm_log_dt, v_ssm_b_re, v_ssm_b_im, v_ssm_c_re, v_ssm_c_im, v_ssm_d, v_glu_w, v_glu_v, v_w_out, v_ln1_g, v_ln1_b, v_mlp_w_up, v_mlp_b_up, v_mlp_w_down, v_mlp_b_down, v_ln2_g, v_ln2_b):
    local = dict(locals())
    s, d = x.shape[1], x.shape[2]
    dr, ds_ = d, d // 2
    din4 = w_in.shape[2]
    din = N_CHIPS * din4
    df4 = mlp_w_up.shape[2]
    df = N_CHIPS * df4
    d4 = d // N_CHIPS
    nh, hd = rg_wa.shape[1], rg_wa.shape[2]
    grp, pst, gh = ssm_b_re.shape[1], ssm_b_re.shape[2], ssm_b_re.shape[3]
    tch = min(s, 256)
    tm = min(s, 1024)
    tmr = min(s, 256)
    xi, yi, ci = _me()
    c_arr = ci.reshape(1).astype(jnp.int32)
    chip_arr = (2 * xi + yi).reshape(1).astype(jnp.int32)
    sel_arr = jnp.stack([2 * xi + yi, ci]).astype(jnp.int32)

    x2 = x[0]
    tgt = loss_target[0]
    row = lambda a: a.reshape(1, -1)

    def gather(name, w):
        r, c = w.shape
        wb = _cast_bf16("cast_" + name, w, chip_arr)
        return _all_gather_chips("ag_" + name, wb.reshape(N_CHIPS, 2, r // 2, c)).reshape(N_CHIPS, r, c)

    w_in_s = gather("w_in", w_in[0])
    w_a_out_f = gather("w_a_out", w_a_out[0]).reshape(dr, d)
    glu_w_s = gather("glu_w", glu_w[0])
    glu_v_s = gather("glu_v", glu_v[0])
    w_out_f = gather("w_out", w_out[0]).reshape(d, d)
    w_up_s = gather("mlp_w_up", mlp_w_up[0])
    w_down_f = gather("mlp_w_down", mlp_w_down[0]).reshape(df, d)

    cw_place = jnp.zeros((4, dr), F32)
    cw_place = lax.dynamic_update_slice(cw_place, conv_w[0] * (ci == 0).astype(F32), (0, (2 * xi + yi) * d4))
    cw_rows = max(8, (4 * dr) // (8 * PACK_W))
    cw_pad = jnp.zeros((8 * cw_rows * PACK_W,), F32).at[:4 * dr].set(cw_place.reshape(-1))
    cw_full = _all_reduce_small("ar_conv_w", cw_pad.reshape(8, cw_rows, PACK_W)).reshape(-1)[:4 * dr].reshape(4, dr)

    lam = rg_lambda[0]
    la = row(jax.nn.softplus(-lam))
    wa_b = rg_wa[0].astype(BF16)
    wx_b = rg_wx[0].astype(BF16)
    disc_in = (ssm_a_re[0], ssm_a_im[0], ssm_log_dt[0], ssm_b_re[0], ssm_b_im[0])
    (lbr, lbi, bbr, bbi), disc_vjp = jax.vjp(_s5_disc, *disc_in)
    bre_m, bim_m = _s5_bmat(bbr).astype(BF16), _s5_bmat(bbi).astype(BF16)
    cre_m, cim_m = _s5_cmat(ssm_c_re[0]).astype(BF16), _s5_cmat(ssm_c_im[0]).astype(BF16)
    to_tile = lambda a: a.reshape(SCAN_ROWS, -1, LANES).transpose(1, 0, 2)
    from_tile = lambda a: a.transpose(1, 0, 2).reshape(grp, pst)
    lbr_t, lbi_t = to_tile(lbr), to_tile(lbi)
    dvec = row(ssm_d[0])

    tn_in = d // 8
    n4 = din4 // tn_in
    z, = _mm("in_proj", x2, w_in_s,
             pl.BlockSpec((tm, d), lambda i, j, k: (i, 0)),
             pl.BlockSpec((None, d, tn_in), lambda i, j, k: (j // n4, 0, j % n4)),
             NN, (s // tm, din // tn_in, 1), [_sds((s, din), F32)],
             [pl.BlockSpec((tm, tn_in), lambda i, j, k: (i, j))], _store())
    ucol = (2 * dr) // ds_
    hsave, hg = _rglru_fwd(z, cw_full, row(conv_b[0]), wa_b, row(rg_ba[0]), wx_b, row(rg_bx[0]), la, s, dr, tch)
    ys, yg, hre, him = _s5_fwd(z, bre_m, bim_m, cre_m, cim_m, lbr_t, lbi_t, dvec, s, ds_, tch, ucol)

    tn4 = d4
    ya, = _mm("ya", hg, w_a_out_f,
              pl.BlockSpec((tm, dr), lambda i, j, k: (i, 0)), pl.BlockSpec((dr, tn4), lambda i, j, k: (0, j)),
              NN, (s // tm, d // tn4, 1), [_sds((s, d), F32)], [pl.BlockSpec((tm, tn4), lambda i, j, k: (i, j))], _store())
    glu_spec = pl.BlockSpec((None, ds_, tn4), lambda i, j, k: (j, 0, 0))
    pp, = _mm("glu_p", yg, glu_w_s, pl.BlockSpec((tm, ds_), lambda i, j, k: (i, 0)), glu_spec,
              NN, (s // tm, N_CHIPS, 1), [_sds((s, d), F32)], [pl.BlockSpec((tm, tn4), lambda i, j, k: (i, j))], _store())
    col_ga = (2 * dr + ds_) // tn4
    col_gb = col_ga + d // tn4

    def mix_epi(acc, ex, outs):
        p_ref, ya_ref, ga_ref, gb_ref = ex
        outs[0][...] = acc
        mix = jax.nn.sigmoid(ga_ref[...]) * ya_ref[...] + jax.nn.sigmoid(gb_ref[...]) * (p_ref[...] * jax.nn.sigmoid(acc))
        outs[1][...] = mix.astype(BF16)

    blk4 = pl.BlockSpec((tm, tn4), lambda i, j, k: (i, j))
    qq, mixb = _mm("glu_q_mix", yg, glu_v_s, pl.BlockSpec((tm, ds_), lambda i, j, k: (i, 0)), glu_spec,
                   NN, (s // tm, N_CHIPS, 1), [_sds((s, d), F32), _sds((s, d), BF16)], [blk4, blk4], mix_epi,
                   extras=(pp, ya, z, z),
                   extra_specs=(blk4, blk4, pl.BlockSpec((tm, tn4), lambda i, j, k: (i, col_ga + j)),
                                pl.BlockSpec((tm, tn4), lambda i, j, k: (i, col_gb + j))))

    g1, b1, g2, b2 = row(ln1_g[0]), row(ln1_b[0]), row(ln2_g[0]), row(ln2_b[0])
    vec = lambda n: pl.BlockSpec((1, n), lambda i, j, k: (0, 0))
    rowblk = pl.BlockSpec((tmr, d), lambda i, j, k: (i, 0))
    colblk1 = pl.BlockSpec((tmr, 1), lambda i, j, k: (i, 0))
    part_blk = pl.BlockSpec((8, d), lambda i, j, k: (i, 0))

    def ln1_epi(acc, ex, outs):
        x_ref, g_ref, b_ref = ex
        r1 = ALPHA * x_ref[...] + acc
        mu = jnp.mean(r1, axis=-1, keepdims=True)
        cen = r1 - mu
        var = jnp.mean(cen * cen, axis=-1, keepdims=True)
        rstd = lax.rsqrt(var + LN_EPS)
        xh = cen * rstd
        outs[0][...] = xh
        outs[1][...] = (xh * g_ref[...] + b_ref[...]).astype(BF16)
        outs[2][...] = rstd

    xhat1, x1b, rstd1 = _mm("out_proj_ln1", mixb, w_out_f, rowblk, pl.BlockSpec((d, d), lambda i, j, k: (0, 0)),
                            NN, (s // tmr, 1, 1), [_sds((s, d), F32), _sds((s, d), BF16), _sds((s, 1), F32)],
                            [rowblk, rowblk, colblk1], ln1_epi, extras=(x2, g1, b1), extra_specs=(rowblk, vec(d), vec(d)))

    tnf = min(df4, 1024)
    nf4 = df4 // tnf

    def up_epi(acc, ex, outs):
        hp = acc + ex[0][...]
        rl = jnp.maximum(hp, 0.0)
        outs[0][...] = (rl * rl).astype(BF16)
        outs[1][...] = rl.astype(BF16)

    fblk = pl.BlockSpec((tm, tnf), lambda i, j, k: (i, j))
    hact, hrelu = _mm("mlp_up", x1b, w_up_s, pl.BlockSpec((tm, d), lambda i, j, k: (i, 0)),
                      pl.BlockSpec((None, d, tnf), lambda i, j, k: (j // nf4, 0, j % nf4)),
                      NN, (s // tm, df // tnf, 1), [_sds((s, df), BF16), _sds((s, df), BF16)], [fblk, fblk], up_epi,
                      extras=(row(mlp_b_up[0]),), extra_specs=(pl.BlockSpec((1, tnf), lambda i, j, k: (0, j)),))

    tkd = min(df, 1024)

    def down_epi(acc, ex, outs):
        xh1_ref, t_ref, g1_ref, b1_ref, g2_ref, b2_ref, bd_ref = ex
        x1 = xh1_ref[...] * g1_ref[...] + b1_ref[...]
        r2 = ALPHA * x1 + (acc + bd_ref[...])
        mu = jnp.mean(r2, axis=-1, keepdims=True)
        cen = r2 - mu
        var = jnp.mean(cen * cen, axis=-1, keepdims=True)
        rstd = lax.rsqrt(var + LN_EPS)
        xh2 = cen * rstd
        err = (xh2 * g2_ref[...] + b2_ref[...]) - t_ref[...]
        tot = 0.5 * jnp.sum(jnp.mean(err * err, axis=-1, keepdims=True))
        rows_i = lax.broadcasted_iota(jnp.int32, (8, 128), 0)
        cols_i = lax.broadcasted_iota(jnp.int32, (8, 128), 1)
        outs[5][...] = jnp.where((rows_i == 0) & (cols_i == 0), tot, 0.0)
        dy = err * (1.0 / d)
        outs[2][...] = _rows_part(dy * xh2)
        outs[3][...] = _rows_part(dy)
        dxh = dy * g2_ref[...]
        m1 = jnp.mean(dxh, axis=-1, keepdims=True)
        m2 = jnp.mean(dxh * xh2, axis=-1, keepdims=True)
        dr2 = rstd * (dxh - m1 - xh2 * m2)
        outs[0][...] = dr2
        outs[1][...] = dr2.astype(BF16)
        outs[4][...] = _rows_part(dr2)

    nrb = s // tmr
    dr2, dr2b, dg2p, db2p, dbdp, lossp = _mm(
        "mlp_down_ln2_loss", hact, w_down_f, pl.BlockSpec((tmr, tkd), lambda i, j, k: (i, k)),
        pl.BlockSpec((tkd, d), lambda i, j, k: (k, 0)), NN, (nrb, 1, df // tkd),
        [_sds((s, d), F32), _sds((s, d), BF16), _sds((nrb * 8, d), F32), _sds((nrb * 8, d), F32),
         _sds((nrb * 8, d), F32), _sds((nrb * 8, 128), F32)],
        [rowblk, rowblk, part_blk, part_blk, part_blk, pl.BlockSpec((8, 128), lambda i, j, k: (i, 0))], down_epi,
        extras=(xhat1, tgt, g1, b1, g2, b2, row(mlp_b_down[0])),
        extra_specs=(rowblk, rowblk, vec(d), vec(d), vec(d), vec(d), vec(d)), acc_shape=(tmr, d))

    def dh_epi(acc, ex, outs):
        dh = acc * (2.0 * ex[0][...].astype(F32))
        outs[0][...] = dh.astype(BF16)
        outs[1][...] = _rows_part(dh)

    ntb = s // tm
    dhpre, dbup_p = _mm("mlp_down_bwd", dr2b, w_down_f, pl.BlockSpec((tm, d), lambda i, j, k: (i, 0)),
                        pl.BlockSpec((tnf, d), lambda i, j, k: (j, 0)), NT, (ntb, df // tnf, 1),
                        [_sds((s, df), BF16), _sds((ntb * 8, df), F32)],
                        [fblk, pl.BlockSpec((8, tnf), lambda i, j, k: (i, j))], dh_epi,
                        extras=(hrelu,), extra_specs=(fblk,))

    tt = min(s, 512)
    ntt = s // tt
    tkk = min(d, 1024)
    both = lambda shape: [_sds(shape, F32), _sds(shape, BF16)]
    sq_blk = pl.BlockSpec((tkk, tkk), lambda i, j, k: (i, j))
    g_down = _mm("grad_w_down", hact, dr2b, pl.BlockSpec((tt, tkk), lambda i, j, k: (k, i)),
                 pl.BlockSpec((tt, tkk), lambda i, j, k: (k, j)), TN, (df // tkk, d // tkk, ntt),
                 both((df, d)), [sq_blk, sq_blk], _store_both, acc_shape=(tkk, tkk))
    up_blk = pl.BlockSpec((None, tkk, tnf), lambda i, j, k: (j // nf4, i, j % nf4))
    g_up = _mm("grad_w_up", x1b, dhpre, pl.BlockSpec((tt, tkk), lambda i, j, k: (k, i)),
               pl.BlockSpec((tt, tnf), lambda i, j, k: (k, j)), TN, (d // tkk, df // tnf, ntt),
               both((N_CHIPS, d, df4)), [up_blk, up_blk], _store_both, acc_shape=(tkk, tnf))

    def ln1_bwd_epi(acc, ex, outs):
        dr2_ref, xh_ref, rs_ref, g_ref = ex
        dx1 = ALPHA * dr2_ref[...] + acc
        xh = xh_ref[...]
        outs[2][...] = _rows_part(dx1 * xh)
        outs[3][...] = _rows_part(dx1)
        dxh = dx1 * g_ref[...]
        m1 = jnp.mean(dxh, axis=-1, keepdims=True)
        m2 = jnp.mean(dxh * xh, axis=-1, keepdims=True)
        dr1 = rs_ref[...] * (dxh - m1 - xh * m2)
        outs[0][...] = dr1
        outs[1][...] = dr1.astype(BF16)

    dr1, dr1b, dg1p, db1p = _mm(
        "mlp_up_bwd_ln1_bwd", dhpre, w_up_s, pl.BlockSpec((tmr, tnf), lambda i, j, k: (i, k)),
        pl.BlockSpec((None, d, tnf), lambda i, j, k: (k // nf4, 0, k % nf4)), NT, (nrb, 1, df // tnf),
        [_sds((s, d), F32), _sds((s, d), BF16), _sds((nrb * 8, d), F32), _sds((nrb * 8, d), F32)],
        [rowblk, rowblk, part_blk, part_blk], ln1_bwd_epi,
        extras=(dr2, xhat1, rstd1, g1), extra_specs=(rowblk, rowblk, colblk1, vec(d)), acc_shape=(tmr, d))

    dmix, = _mm("out_proj_bwd", dr1b, w_out_f, pl.BlockSpec((tm, d), lambda i, j, k: (i, 0)),
                pl.BlockSpec((tn4, d), lambda i, j, k: (j, 0)), NT, (ntb, d // tn4, 1),
                [_sds((s, d), BF16)], [blk4], _store())
    g_out = _mm("grad_w_out", mixb, dr1b, pl.BlockSpec((tt, tkk), lambda i, j, k: (k, i)),
                pl.BlockSpec((tt, tkk), lambda i, j, k: (k, j)), TN, (d // tkk, d // tkk, ntt),
                both((d, d)), [sq_blk, sq_blk], _store_both, acc_shape=(tkk, tkk))

    def dya_body(dm_ref, g_ref, o_ref):
        o_ref[...] = (dm_ref[...].astype(F32) * jax.nn.sigmoid(g_ref[...])).astype(BF16)

    tme = min(s, 512)
    eblk = pl.BlockSpec((tme, tn4), lambda i, j: (i, j))
    dya = _pcall(dya_body, grid=(s // tme, d // tn4),
                 in_specs=[eblk, pl.BlockSpec((tme, tn4), lambda i, j: (i, col_ga + j))], out_specs=eblk,
                 out_shape=_sds((s, d), BF16), compiler_params=_cparams(2), name="dya")(dmix, z)
    g_a_out = _mm("grad_w_a_out", hg, dya, pl.BlockSpec((tt, tkk), lambda i, j, k: (k, i)),
                  pl.BlockSpec((tt, tkk), lambda i, j, k: (k, j)), TN, (dr // tkk, d // tkk, ntt),
                  both((dr, d)), [sq_blk, sq_blk], _store_both, acc_shape=(tkk, tkk))
    dhg, = _mm("a_out_bwd", dya, w_a_out_f, pl.BlockSpec((tm, d), lambda i, j, k: (i, 0)),
               pl.BlockSpec((tn4, d), lambda i, j, k: (j, 0)), NT, (ntb, dr // tn4, 1),
               [_sds((s, dr), F32)], [blk4], _store())
    dz, dcw_p, dcb_p, dwa, dwx, dba_p, dbx_p, dla_p = _rglru_bwd(
        dhg, z, hsave, cw_full, row(conv_b[0]), wa_b, row(rg_ba[0]), wx_b, row(rg_bx[0]), la, s, dr, din, tch)
    dz = _gate_a_bwd(dmix, ya, z, dz, s, d, tn4, col_ga)
    dz, dpb, dqb = _gate_b_bwd(dmix, pp, qq, z, dz, s, d, tn4, col_gb)

    glu_gblk = pl.BlockSpec((None, ds_, tn4), lambda i, j, k: (j, 0, 0))
    g_glu_w = _mm("grad_glu_w", yg, dpb, pl.BlockSpec((tt, ds_), lambda i, j, k: (k, 0)),
                  pl.BlockSpec((tt, tn4), lambda i, j, k: (k, j)), TN, (1, N_CHIPS, ntt),
                  both((N_CHIPS, ds_, d4)), [glu_gblk, glu_gblk], _store_both, acc_shape=(ds_, tn4))
    g_glu_v = _mm("grad_glu_v", yg, dqb, pl.BlockSpec((tt, ds_), lambda i, j, k: (k, 0)),
                  pl.BlockSpec((tt, tn4), lambda i, j, k: (k, j)), TN, (1, N_CHIPS, ntt),
                  both((N_CHIPS, ds_, d4)), [glu_gblk, glu_gblk], _store_both, acc_shape=(ds_, tn4))
    sblk = pl.BlockSpec((tm, ds_), lambda i, j, k: (i, 0))
    glu_bspec = pl.BlockSpec((None, ds_, tn4), lambda i, j, k: (k, 0, 0))
    dyg_p, = _mm("glu_w_bwd", dpb, glu_w_s, pl.BlockSpec((tm, tn4), lambda i, j, k: (i, k)), glu_bspec,
                 NT, (ntb, 1, N_CHIPS), [_sds((s, ds_), F32)], [sblk], _store(), acc_shape=(tm, ds_))

    def dys_epi(acc, ex, outs):
        outs[0][...] = (acc + ex[0][...]) * _gelu_grad(ex[1][...])

    dys, = _mm("glu_v_bwd_gelu_bwd", dqb, glu_v_s, pl.BlockSpec((tm, tn4), lambda i, j, k: (i, k)), glu_bspec,
               NT, (ntb, 1, N_CHIPS), [_sds((s, ds_), F32)], [sblk], dys_epi,
               extras=(dyg_p, ys), extra_specs=(sblk, sblk), acc_shape=(tm, ds_))
    dz, dbre, dbim, dcre, dcim, dlr, dli, ddp = _s5_bwd(dys, z, hre, him, bre_m, bim_m, cre_m, cim_m, lbr_t, lbi_t,
                                                      dvec, dz, s, ds_, tch, ucol)

    tni = din4 // 3
    ni4 = 3
    in_gblk = pl.BlockSpec((None, tkk, tni), lambda i, j, k: (j // ni4, i, j % ni4))
    g_in = _mm("grad_w_in", x2, dz, pl.BlockSpec((tt, tkk), lambda i, j, k: (k, i)),
               pl.BlockSpec((tt, tni), lambda i, j, k: (k, j)), TN, (d // tkk, din // tni, ntt),
               both((N_CHIPS, d, din4)), [in_gblk, in_gblk], _store_both, acc_shape=(tkk, tni))

    def dx_epi(acc, ex, outs):
        outs[0][...] = ALPHA * ex[0][...] + acc

    tmx = min(s, 512)
    xblk = pl.BlockSpec((tmx, d), lambda i, j, k: (i, 0))
    grad_x, = _mm("in_proj_bwd", dz, w_in_s, pl.BlockSpec((tmx, tni), lambda i, j, k: (i, k)),
                  pl.BlockSpec((None, d, tni), lambda i, j, k: (k // ni4, 0, k % ni4)), NT, (s // tmx, 1, din // tni),
                  [_sds((s, d), F32)], [xblk], dx_epi, extras=(dr1,), extra_specs=(xblk,), acc_shape=(tmx, d))

    stack = lambda g, r: g.reshape(N_CHIPS, 2, r // 2, g.shape[-1])
    big = [("w_in", g_in, d), ("w_a_out", g_a_out, dr // N_CHIPS), ("glu_w", g_glu_w, ds_), ("glu_v", g_glu_v, ds_),
           ("w_out", g_out, d4), ("mlp_w_up", g_up, d), ("mlp_w_down", g_down, df4)]
    names_big = [n for n, _, _ in big]
    g32 = [stack(g[0], r) for _, g, r in big]
    g16 = [stack(g[1], r) for _, g, r in big]
    recv1 = _rs_to_sibling("rs_sibling", g16)
    hsum = [_add_half("rs_add_half_" + n, g, r, c_arr) for n, g, r in zip(names_big, g32, recv1)]
    recv2 = _rs_to_chips("rs_chips", [h[1] for h in hsum])
    rsum = [_add_chips("rs_add_chips_" + n, h[0], r, sel_arr) for n, h, r in zip(names_big, hsum, recv2)]
    joined = _rs_join("rs_join", rsum)
    res = {}
    for n, gj in zip(names_big, joined):
        w = local[n][0]
        g = gj.reshape(w.shape)
        dl, nm, nv = _adamw("adamw_" + n, w, g, local["m_" + n][0], local["v_" + n][0])
        res[n] = (g[None], dl[None], nm[None], nv[None])

    s8 = lambda p: p.sum(axis=0)
    red_names = ["loss", "conv_w", "conv_b", "rg_wa", "rg_ba", "rg_wx", "rg_bx", "dla", "lbr", "lbi", "bbr", "bbi",
                 "c_re", "c_im", "ssm_d", "ln1_g", "ln1_b", "mlp_b_up", "mlp_b_down", "ln2_g", "ln2_b"]
    red_vals = [lossp.sum().reshape(1), dcw_p.reshape(4, 8, dr).sum(axis=1), s8(dcb_p), dwa, s8(dba_p), dwx, s8(dbx_p),
                s8(dla_p), from_tile(dlr), from_tile(dli), _s5_bmat_diag(dbre, grp, pst, gh),
                _s5_bmat_diag(dbim, grp, pst, gh), _s5_cmat_diag(dcre, grp, gh, pst), _s5_cmat_diag(dcim, grp, gh, pst),
                s8(ddp).reshape(grp, gh), s8(dg1p), s8(db1p), s8(dbup_p), s8(dbdp), s8(dg2p), s8(db2p)]
    red_shapes = [v.shape for v in red_vals]
    rows_r = _pack_rows(red_shapes)
    red = _all_reduce_small("ar_small", _pack(red_vals, rows_r).reshape(8, rows_r // 8, PACK_W))
    rv = dict(zip(red_names, _unpack(red.reshape(rows_r, PACK_W), red_shapes)))
    loss = rv["loss"][0]
    d_are, d_aim, d_ldt, d_bre, d_bim = disc_vjp((rv["lbr"], rv["lbi"], rv["bbr"], rv["bbi"]))
    small_g = {"conv_w": rv["conv_w"], "conv_b": rv["conv_b"], "rg_wa": rv["rg_wa"], "rg_ba": rv["rg_ba"].reshape(nh, hd),
               "rg_wx": rv["rg_wx"], "rg_bx": rv["rg_bx"].reshape(nh, hd), "rg_lambda": rv["dla"] * (-jax.nn.sigmoid(-lam)),
               "ssm_a_re": d_are, "ssm_a_im": d_aim, "ssm_log_dt": d_ldt, "ssm_b_re": d_bre, "ssm_b_im": d_bim,
               "ssm_c_re": rv["c_re"], "ssm_c_im": rv["c_im"], "ssm_d": rv["ssm_d"], "ln1_g": rv["ln1_g"],
               "ln1_b": rv["ln1_b"], "mlp_b_up": rv["mlp_b_up"], "mlp_b_down": rv["mlp_b_down"],
               "ln2_g": rv["ln2_g"], "ln2_b": rv["ln2_b"]}
    small_names = list(small_g)
    col0 = (2 * xi + yi) * d4

    def placed(n, a):
        if n != "conv_w":
            return a[0]
        return lax.dynamic_update_slice(jnp.zeros((4, dr), F32), a[0], (0, col0))

    sm_shapes = [small_g[n].shape for n in small_names]
    rows_s = _pack_rows(sm_shapes)
    packs = [_pack([small_g[n] for n in small_names], rows_s)]
    for pre in ("", "m_", "v_"):
        packs.append(_pack([placed(n, local[pre + n]) for n in small_names], rows_s))
    dl_p, nm_p, nv_p = _adamw("adamw_small", packs[1], packs[0], packs[2], packs[3])
    dl_s = dict(zip(small_names, _unpack(dl_p, sm_shapes)))
    nm_s = dict(zip(small_names, _unpack(nm_p, sm_shapes)))
    nv_s = dict(zip(small_names, _unpack(nv_p, sm_shapes)))
    for n in small_names:
        tup = (small_g[n], dl_s[n], nm_s[n], nv_s[n])
        if n == "conv_w":
            tup = tuple(lax.dynamic_slice(a, (0, col0), (4, d4)) for a in tup)
        res[n] = tuple(a.reshape(local[n].shape) for a in tup)

    order = ["w_in", "conv_w", "conv_b", "rg_wa", "rg_ba", "rg_wx", "rg_bx", "rg_lambda", "w_a_out", "ssm_a_re",
             "ssm_a_im", "ssm_log_dt", "ssm_b_re", "ssm_b_im", "ssm_c_re", "ssm_c_im", "ssm_d", "glu_w", "glu_v",
             "w_out", "ln1_g", "ln1_b", "mlp_w_up", "mlp_b_up", "mlp_w_down", "mlp_b_down", "ln2_g", "ln2_b"]
    outs = [loss, grad_x[None]]
    for part in range(4):
        outs += [res[n][part] for n in order]
    return tuple(outs)
```

```python
import functools
import math

import jax
import jax.numpy as jnp
from jax import lax
from jax.experimental import pallas as pl
from jax.experimental.pallas import tpu as pltpu

F32 = jnp.float32
BF16 = jnp.bfloat16
MESH = pl.DeviceIdType.MESH
ANY = pl.BlockSpec(memory_space=pl.ANY)

ALPHA = 2.0 ** 0.25
LN_EPS = 1e-5
RG_C = 8.0
ADAM_LR, ADAM_B1, ADAM_B2, ADAM_EPS, ADAM_WD, ADAM_STEP = 0.001, 0.9, 0.999, 1e-08, 0.01, 10
GELU_K0 = math.sqrt(2.0 / math.pi)
GELU_K1 = 0.044715
VMEM_LIMIT = 56 * 1024 * 1024
N_CHIPS = 4
SCAN_ROWS = 8


def _pcall(body, **kw):
    return pl.pallas_call(body, **kw)


def _cparams(n_grid):
    return pltpu.CompilerParams(dimension_semantics=("arbitrary",) * n_grid, vmem_limit_bytes=VMEM_LIMIT)


def _gelu(x):
    return 0.5 * x * (1.0 + jnp.tanh(GELU_K0 * (x + GELU_K1 * x * x * x)))


def _gelu_grad(x):
    t = jnp.tanh(GELU_K0 * (x + GELU_K1 * x * x * x))
    return 0.5 * (1.0 + t) + 0.5 * x * (1.0 - t * t) * GELU_K0 * (1.0 + 3.0 * GELU_K1 * x * x)


def _neg_expm1(x):
    series = x * (1.0 + x * (0.5 + x * (1.0 / 6.0 + x * (1.0 / 24.0))))
    return -jnp.where(x > -0.03, series, jnp.exp(x) - 1.0)


def _rows_part(v):
    m, n = v.shape
    return v.reshape(m // 8, 8, n).sum(axis=0)


def _dot(a, b):
    return jnp.dot(a, b, preferred_element_type=F32)


def _dot_nt(a, b):
    return lax.dot_general(a, b, (((1,), (1,)), ((), ())), preferred_element_type=F32)


def _dot_tn(a, b):
    return lax.dot_general(a, b, (((0,), (0,)), ((), ())), preferred_element_type=F32)


NN = (((1,), (0,)), ((), ()))
NT = (((1,), (1,)), ((), ()))
TN = (((0,), (0,)), ((), ()))


def _mm(name, a, b, a_spec, b_spec, contract, grid, out_shape, out_specs, epilogue,
        extras=(), extra_specs=(), acc_shape=None, aliases=None):
    nk = grid[2]
    ne = len(extras)
    n_out = len(out_shape)

    def body(*refs):
        a_ref, b_ref = refs[0], refs[1]
        ex = refs[2:2 + ne]
        outs = refs[2 + ne:2 + ne + n_out]
        prod = lax.dot_general(a_ref[...].astype(BF16), b_ref[...].astype(BF16), contract,
                               preferred_element_type=F32)
        if nk == 1:
            epilogue(prod, ex, outs)
        else:
            acc = refs[2 + ne + n_out]
            k = pl.program_id(2)

            @pl.when(k == 0)
            def _():
                acc[...] = prod

            @pl.when(k > 0)
            def _():
                acc[...] += prod

            @pl.when(k == nk - 1)
            def _():
                epilogue(acc[...], ex, outs)

    scratch = [pltpu.VMEM(acc_shape, F32)] if nk > 1 else []
    return _pcall(body, grid=grid, in_specs=[a_spec, b_spec, *extra_specs], out_specs=list(out_specs),
                  out_shape=list(out_shape), scratch_shapes=scratch, input_output_aliases=aliases or {},
                  compiler_params=_cparams(3), name=name)(a, b, *extras)


def _store(dtype=None):
    def epi(acc, ex, outs):
        outs[0][...] = acc.astype(outs[0].dtype)
    return epi


def _store_both(acc, ex, outs):
    outs[0][...] = acc
    outs[1][...] = acc.astype(BF16)


def _sds(shape, dtype):
    return jax.ShapeDtypeStruct(tuple(shape), dtype)


def _row_tile(rows, cols, budget=1 << 20):
    t = rows
    while t * cols * 4 > budget and t % 16 == 0:
        t //= 2
    return t


def _cast_bf16(name, w, chip_arr):
    r, c = w.shape
    tr = _row_tile(r, c)

    def body(k_ref, w_ref, o_ref):
        o_ref[...] = w_ref[...].astype(BF16)

    gs = pltpu.PrefetchScalarGridSpec(
        num_scalar_prefetch=1, grid=(r // tr,),
        in_specs=[pl.BlockSpec((tr, c), lambda i, kk: (i, 0))],
        out_specs=pl.BlockSpec((None, tr, c), lambda i, kk: (kk[0], i, 0)))
    return _pcall(body, grid_spec=gs, out_shape=_sds((N_CHIPS, r, c), BF16),
                  compiler_params=_cparams(1), name=name)(chip_arr, w)


def _adamw(name, w, g, m, v):
    r, c = w.shape
    tr = _row_tile(r, c)

    def body(w_ref, g_ref, m_ref, v_ref, d_ref, nm_ref, nv_ref):
        gg = g_ref[...]
        nm = ADAM_B1 * m_ref[...] + (1.0 - ADAM_B1) * gg
        nv = ADAM_B2 * v_ref[...] + (1.0 - ADAM_B2) * (gg * gg)
        m_hat = nm / (1.0 - ADAM_B1 ** ADAM_STEP)
        v_hat = nv / (1.0 - ADAM_B2 ** ADAM_STEP)
        d_ref[...] = -ADAM_LR * (m_hat / (jnp.sqrt(v_hat) + ADAM_EPS) + ADAM_WD * w_ref[...])
        nm_ref[...] = nm
        nv_ref[...] = nv

    spec = pl.BlockSpec((tr, c), lambda i: (i, 0))
    return _pcall(body, grid=(r // tr,), in_specs=[spec] * 4, out_specs=[spec] * 3,
                  out_shape=[_sds((r, c), F32)] * 3, compiler_params=_cparams(1), name=name)(w, g, m, v)


def _add_half(name, g, recv, c_arr):
    _, _, rh, c = g.shape
    tr = _row_tile(rh, c)

    def body(c_ref, g_ref, r_ref, o_ref, ob_ref):
        h = g_ref[...] + r_ref[...].astype(F32)
        o_ref[...] = h
        ob_ref[...] = h.astype(BF16)

    spec3 = pl.BlockSpec((None, tr, c), lambda s, r, cc: (s, r, 0))
    gs = pltpu.PrefetchScalarGridSpec(
        num_scalar_prefetch=1, grid=(N_CHIPS, rh // tr),
        in_specs=[pl.BlockSpec((None, None, tr, c), lambda s, r, cc: (s, cc[0], r, 0)), spec3],
        out_specs=[spec3, spec3])
    return _pcall(body, grid_spec=gs, out_shape=[_sds((N_CHIPS, rh, c), F32), _sds((N_CHIPS, rh, c), BF16)],
                  compiler_params=_cparams(2), name=name)(c_arr, g, recv)


def _add_chips(name, h, recv, sel_arr):
    _, rh, c = h.shape
    tr = _row_tile(rh, c)

    def body(k_ref, h_ref, r0, r1, r2, o_ref):
        o_ref[...] = ((h_ref[...] + r0[...].astype(F32)) + r1[...].astype(F32)) + r2[...].astype(F32)

    def rspec(k):
        return pl.BlockSpec((None, tr, c), lambda r, kk, k=k: (k, r, 0))

    gs = pltpu.PrefetchScalarGridSpec(
        num_scalar_prefetch=1, grid=(rh // tr,),
        in_specs=[pl.BlockSpec((None, tr, c), lambda r, kk: (kk[0], r, 0)), rspec(0), rspec(1), rspec(2)],
        out_specs=pl.BlockSpec((None, tr, c), lambda r, kk: (kk[1], r, 0)))
    return _pcall(body, grid_spec=gs, out_shape=_sds((2, rh, c), F32),
                  compiler_params=_cparams(1), name=name)(sel_arr, h, recv, recv, recv)


def _me():
    return lax.axis_index("x"), lax.axis_index("y"), lax.axis_index("c")


def _peer_chip(x, y, k):
    px = (x + (k >> 1)) % 2
    py = (y + (k & 1)) % 2
    return px, py, 2 * px + py


def _all_gather_chips(name, w2):
    _, _, rh, c_ = w2.shape

    def body(in_ref, out_ref, send_sems, recv_sems):
        x, y, c = _me()
        chip = 2 * x + y
        sib = (x, y, 1 - c)
        started = []
        for k in (1, 2, 3):
            px, py, _ = _peer_chip(x, y, k)
            cp = pltpu.make_async_remote_copy(src_ref=out_ref.at[chip, c], dst_ref=out_ref.at[chip, c],
                                              send_sem=send_sems.at[k - 1], recv_sem=recv_sems.at[k - 1],
                                              device_id=(px, py, c), device_id_type=MESH)
            cp.start()
            started.append(cp)
        for k in (1, 2, 3):
            _, _, pchip = _peer_chip(x, y, k)
            pltpu.make_async_remote_copy(src_ref=out_ref.at[chip, c], dst_ref=out_ref.at[pchip, c],
                                         send_sem=send_sems.at[k - 1], recv_sem=recv_sems.at[k - 1],
                                         device_id=sib, device_id_type=MESH).wait_recv()
            fw = pltpu.make_async_remote_copy(src_ref=out_ref.at[pchip, c], dst_ref=out_ref.at[pchip, c],
                                              send_sem=send_sems.at[2 + k], recv_sem=recv_sems.at[2 + k],
                                              device_id=sib, device_id_type=MESH)
            fw.start()
            started.append(fw)
        for k in (1, 2, 3):
            _, _, pchip = _peer_chip(x, y, k)
            pltpu.make_async_remote_copy(src_ref=out_ref.at[chip, c], dst_ref=out_ref.at[pchip, 1 - c],
                                         send_sem=send_sems.at[2 + k], recv_sem=recv_sems.at[2 + k],
                                         device_id=sib, device_id_type=MESH).wait_recv()
        for cp in started:
            cp.wait_send()

    return _pcall(body, in_specs=[ANY], out_specs=ANY, out_shape=_sds(w2.shape, w2.dtype),
                  scratch_shapes=[pltpu.SemaphoreType.DMA((6,)), pltpu.SemaphoreType.DMA((6,))],
                  input_output_aliases={0: 0}, name=name)(w2)


def _rs_to_sibling(name, gs):
    n = len(gs)

    def body(*refs):
        ins, outs = refs[:n], refs[n:2 * n]
        send_sems, recv_sems = refs[2 * n], refs[2 * n + 1]
        x, y, c = _me()
        sib = (x, y, 1 - c)
        cps = []
        for a in range(n):
            cp = pltpu.make_async_remote_copy(src_ref=ins[a].at[:, 1 - c], dst_ref=outs[a],
                                              send_sem=send_sems.at[a], recv_sem=recv_sems.at[a],
                                              device_id=sib, device_id_type=MESH)
            cp.start()
            cps.append(cp)
        for cp in cps:
            cp.wait_recv()
        for cp in cps:
            cp.wait_send()

    shapes = [_sds((g.shape[0], g.shape[2], g.shape[3]), g.dtype) for g in gs]
    return _pcall(body, in_specs=[ANY] * n, out_specs=[ANY] * n, out_shape=shapes,
                  scratch_shapes=[pltpu.SemaphoreType.DMA((n,)), pltpu.SemaphoreType.DMA((n,))], name=name)(*gs)


def _rs_to_chips(name, hs):
    n = len(hs)

    def body(*refs):
        ins, outs = refs[:n], refs[n:2 * n]
        send_sems, recv_sems = refs[2 * n], refs[2 * n + 1]
        x, y, c = _me()
        cps = []
        for a in range(n):
            for k in (1, 2, 3):
                px, py, pchip = _peer_chip(x, y, k)
                cp = pltpu.make_async_remote_copy(src_ref=ins[a].at[pchip], dst_ref=outs[a].at[k - 1],
                                                  send_sem=send_sems.at[a, k - 1], recv_sem=recv_sems.at[a, k - 1],
                                                  device_id=(px, py, c), device_id_type=MESH)
                cp.start()
                cps.append(cp)
        for cp in cps:
            cp.wait_recv()
        for cp in cps:
            cp.wait_send()

    shapes = [_sds((3, h.shape[1], h.shape[2]), h.dtype) for h in hs]
    return _pcall(body, in_specs=[ANY] * n, out_specs=[ANY] * n, out_shape=shapes,
                  scratch_shapes=[pltpu.SemaphoreType.DMA((n, 3)), pltpu.SemaphoreType.DMA((n, 3))], name=name)(*hs)


def _rs_join(name, rs):
    n = len(rs)

    def body(*refs):
        outs = refs[n:2 * n]
        send_sems, recv_sems = refs[2 * n], refs[2 * n + 1]
        x, y, c = _me()
        sib = (x, y, 1 - c)
        cps = []
        for a in range(n):
            cp = pltpu.make_async_remote_copy(src_ref=outs[a].at[c], dst_ref=outs[a].at[c],
                                              send_sem=send_sems.at[a], recv_sem=recv_sems.at[a],
                                              device_id=sib, device_id_type=MESH)
            cp.start()
            cps.append(cp)
        for a in range(n):
            pltpu.make_async_remote_copy(src_ref=outs[a].at[c], dst_ref=outs[a].at[1 - c],
                                         send_sem=send_sems.at[a], recv_sem=recv_sems.at[a],
                                         device_id=sib, device_id_type=MESH).wait_recv()
        for cp in cps:
            cp.wait_send()

    shapes = [_sds(r.shape, F32) for r in rs]
    return _pcall(body, in_specs=[ANY] * n, out_specs=[ANY] * n, out_shape=shapes,
                  scratch_shapes=[pltpu.SemaphoreType.DMA((n,)), pltpu.SemaphoreType.DMA((n,))],
                  input_output_aliases={a: a for a in range(n)}, name=name)(*rs)


def _all_reduce_small(name, p):
    _, r, w = p.shape

    def body(in_ref, out_ref, recv, send_sems, recv_sems):
        x, y, c = _me()
        me = 4 * x + 2 * y + c

        def peer(k):
            px, py, pc = (x + (k >> 2)) % 2, (y + ((k >> 1) & 1)) % 2, (c + (k & 1)) % 2
            return (px, py, pc), 4 * px + 2 * py + pc

        cps = []
        for k in range(1, 8):
            dev, idx = peer(k)
            cp = pltpu.make_async_remote_copy(src_ref=in_ref.at[idx], dst_ref=recv.at[k],
                                              send_sem=send_sems.at[0, k], recv_sem=recv_sems.at[0, k],
                                              device_id=dev, device_id_type=MESH)
            cp.start()
            cps.append(cp)
        for cp in cps:
            cp.wait_recv()
        acc = in_ref[me]
        for k in range(1, 8):
            acc = acc + recv[k]
        out_ref[me] = acc
        cps2 = []
        for k in range(1, 8):
            dev, idx = peer(k)
            cp = pltpu.make_async_remote_copy(src_ref=out_ref.at[me], dst_ref=out_ref.at[me],
                                              send_sem=send_sems.at[1, k], recv_sem=recv_sems.at[1, k],
                                              device_id=dev, device_id_type=MESH)
            cp.start()
            cps2.append(cp)
        for k in range(1, 8):
            dev, idx = peer(k)
            pltpu.make_async_remote_copy(src_ref=out_ref.at[me], dst_ref=out_ref.at[idx],
                                         send_sem=send_sems.at[1, k], recv_sem=recv_sems.at[1, k],
                                         device_id=dev, device_id_type=MESH).wait_recv()
        for cp in cps + cps2:
            cp.wait_send()

    vm = pl.BlockSpec(memory_space=pltpu.VMEM)
    return _pcall(body, in_specs=[vm], out_specs=vm, out_shape=_sds(p.shape, F32),
                  scratch_shapes=[pltpu.VMEM((8, r, w), F32), pltpu.SemaphoreType.DMA((2, 8)),
                                  pltpu.SemaphoreType.DMA((2, 8))],
                  compiler_params=pltpu.CompilerParams(vmem_limit_bytes=VMEM_LIMIT), name=name)(p)


def _rglru_gates(xc, h, wa_ref, ba_ref, wx_ref, bx_ref, la_ref, hs):
    xb = xc.astype(BF16)
    r = jax.nn.sigmoid(_dot(xb, wa_ref[h]) + ba_ref[:, hs])
    ig = jax.nn.sigmoid(_dot(xb, wx_ref[h]) + bx_ref[:, hs])
    log_a = (-RG_C * r) * la_ref[:, hs]
    a = jnp.exp(log_a)
    mult = jnp.sqrt(_neg_expm1(2.0 * log_a))
    return r, ig, a, mult


def _conv_taps(cw_ref, cb_ref, xext, t, hs):
    acc = cb_ref[:, hs] + cw_ref[0:1, hs] * xext[pl.ds(5, t), hs]
    for k in range(1, 4):
        acc = acc + cw_ref[k:k + 1, hs] * xext[pl.ds(5 + k, t), hs]
    return acc


def _rglru_fwd(z, conv_w, conv_b, wa_b, ba, wx_b, bx, la, s, dr, t):
    nh = wa_b.shape[0]
    hd = dr // nh
    lw = dr // SCAN_ROWS
    hpr = lw // hd

    def body(xr_ref, gt_ref, cw_ref, cb_ref, wa_ref, ba_ref, wx_ref, bx_ref, la_ref, h_ref, hg_ref,
             xext, a_s, b_s, hcar):
        i = pl.program_id(0)

        @pl.when(i == 0)
        def _():
            xext[0:8, :] = jnp.zeros((8, dr), F32)
            hcar[...] = jnp.zeros((hpr, 8, hd), F32)

        @pl.when(i > 0)
        def _():
            xext[0:8, :] = xext[t:t + 8, :]

        xext[8:t + 8, :] = xr_ref[...]
        for h in range(nh):
            hs = slice(h * hd, (h + 1) * hd)
            xc = _conv_taps(cw_ref, cb_ref, xext, t, hs)
            _, ig, a, mult = _rglru_gates(xc, h, wa_ref, ba_ref, wx_ref, bx_ref, la_ref, hs)
            a_s[h % hpr, pl.ds(h // hpr, t, stride=8), :] = a
            b_s[h % hpr, pl.ds(h // hpr, t, stride=8), :] = mult * (ig * xc)

        def step(tt, hp):
            o = pl.multiple_of(tt * 8, 8)
            hn = a_s[:, pl.ds(o, 8), :] * hp + b_s[:, pl.ds(o, 8), :]
            b_s[:, pl.ds(o, 8), :] = hn
            return hn

        hcar[...] = lax.fori_loop(0, t, step, hcar[...], unroll=8)
        for h in range(nh):
            hs = slice(h * hd, (h + 1) * hd)
            hj = b_s[h % hpr, pl.ds(h // hpr, t, stride=8), :]
            h_ref[:, hs] = hj
            hg_ref[:, hs] = (hj * _gelu(gt_ref[:, hs])).astype(BF16)

    full = lambda arr: pl.BlockSpec(arr.shape, lambda i: (0,) * arr.ndim)
    return _pcall(
        body, grid=(s // t,),
        in_specs=[pl.BlockSpec((t, dr), lambda i: (i, 0)), pl.BlockSpec((t, dr), lambda i: (i, 1)),
                  full(conv_w), full(conv_b), full(wa_b), full(ba), full(wx_b), full(bx), full(la)],
        out_specs=[pl.BlockSpec((t, dr), lambda i: (i, 0))] * 2,
        out_shape=[_sds((s, dr), F32), _sds((s, dr), BF16)],
        scratch_shapes=[pltpu.VMEM((t + 8, dr), F32), pltpu.VMEM((hpr, t * 8, hd), F32),
                        pltpu.VMEM((hpr, t * 8, hd), F32), pltpu.VMEM((hpr, 8, hd), F32)],
        compiler_params=_cparams(1), name="rglru_fwd")(z, z, conv_w, conv_b, wa_b, ba, wx_b, bx, la)


def _rglru_bwd(dhg, z, hsave, conv_w, conv_b, wa_b, ba, wx_b, bx, la, s, dr, din, t):
    nh = wa_b.shape[0]
    hd = dr // nh
    lw = dr // SCAN_ROWS
    hpr = lw // hd
    nch = s // t
    tb = t // 8

    def body(dhg_ref, xr_ref, xp_ref, gt_ref, h_ref, hp_ref, cw_ref, cb_ref, wa_ref, ba_ref, wx_ref, bx_ref, la_ref,
             dz_ref, dcw_ref, dcb_ref, dwa_ref, dwx_ref, dba_ref, dbx_ref, dla_ref,
             xext, hext, r_s, i_s, a_s, g_s, dxe, car):
        i = pl.program_id(0)
        first = (nch - 1 - i) == 0

        @pl.when(i == 0)
        def _():
            for ref in (dcw_ref, dcb_ref, dwa_ref, dwx_ref, dba_ref, dbx_ref, dla_ref, car):
                ref[...] = jnp.zeros(ref.shape, F32)
            dxe[t:t + 8, :] = jnp.zeros((8, dr), F32)

        keep = jnp.where(first, 0.0, 1.0)
        xext[0:8, :] = xp_ref[...] * keep
        hext[0:8, :] = hp_ref[...] * keep
        xext[8:t + 8, :] = xr_ref[...]
        hext[8:t + 8, :] = h_ref[...]
        for h in range(nh):
            hs = slice(h * hd, (h + 1) * hd)
            xc = _conv_taps(cw_ref, cb_ref, xext, t, hs)
            r, ig, a, _ = _rglru_gates(xc, h, wa_ref, ba_ref, wx_ref, bx_ref, la_ref, hs)
            r_s[:, hs] = r
            i_s[:, hs] = ig
            a_s[h % hpr, pl.ds(h // hpr, t, stride=8), :] = a
            gate = gt_ref[:, hs]
            dh_out = dhg_ref[:, hs]
            g_s[h % hpr, pl.ds(h // hpr, t, stride=8), :] = dh_out * _gelu(gate)
            dz_ref[:, dr + h * hd:dr + (h + 1) * hd] = (dh_out * h_ref[:, hs] * _gelu_grad(gate)).astype(BF16)

        def step(k, cr):
            o = pl.multiple_of((t - 1 - k) * 8, 8)
            dh = g_s[:, pl.ds(o, 8), :] + cr
            g_s[:, pl.ds(o, 8), :] = dh
            return a_s[:, pl.ds(o, 8), :] * dh

        car[...] = lax.fori_loop(0, t, step, car[...], unroll=8)

        for h in range(nh):
            hs = slice(h * hd, (h + 1) * hd)
            dh = g_s[h % hpr, pl.ds(h // hpr, t, stride=8), :]
            a = a_s[h % hpr, pl.ds(h // hpr, t, stride=8), :]
            r = r_s[:, hs]
            ig = i_s[:, hs]
            xc = _conv_taps(cw_ref, cb_ref, xext, t, hs)
            la_h = la_ref[:, hs]
            mult = jnp.sqrt(_neg_expm1(2.0 * ((-RG_C * r) * la_h)))
            da = dh * hext[pl.ds(7, t), hs]
            dmult = dh * (ig * xc)
            dlog_a = da * a - dmult * (a * a) / mult
            dr_ = dlog_a * (-RG_C * la_h)
            dla_ref[:, hs] += _rows_part(dlog_a * (-RG_C * r))
            dpr = dr_ * r * (1.0 - r)
            dpi = (dh * mult * xc) * ig * (1.0 - ig)
            dprb, dpib, xcb = dpr.astype(BF16), dpi.astype(BF16), xc.astype(BF16)
            dxc = dh * mult * ig + _dot_nt(dprb, wa_ref[h]) + _dot_nt(dpib, wx_ref[h])
            dwa_ref[h] += _dot_tn(xcb, dprb)
            dwx_ref[h] += _dot_tn(xcb, dpib)
            dba_ref[:, hs] += _rows_part(dpr)
            dbx_ref[:, hs] += _rows_part(dpi)
            dcb_ref[:, hs] += _rows_part(dxc)
            dxe[0:t, hs] = dxc
            for k in range(4):
                dcw_ref[8 * k:8 * k + 8, hs] += _rows_part(dxc * xext[pl.ds(5 + k, t), hs])
        for h in range(nh):
            hs = slice(h * hd, (h + 1) * hd)
            dxr = cw_ref[3:4, hs] * dxe[pl.ds(0, t), hs]
            for k in range(3):
                dxr = dxr + cw_ref[k:k + 1, hs] * dxe[pl.ds(3 - k, t), hs]
            dz_ref[:, hs] = dxr.astype(BF16)
        dxe[t:t + 8, :] = dxe[0:8, :]

    full = lambda arr: pl.BlockSpec(arr.shape, lambda i: (0,) * arr.ndim)
    rev = lambda col: (lambda i: (nch - 1 - i, col))
    prev = lambda i: (jnp.maximum((nch - 1 - i) * tb - 1, 0), 0)
    acc = lambda shape: pl.BlockSpec(shape, lambda i: (0,) * len(shape))
    return _pcall(
        body, grid=(nch,),
        in_specs=[pl.BlockSpec((t, dr), rev(0)), pl.BlockSpec((t, dr), rev(0)), pl.BlockSpec((8, dr), prev),
                  pl.BlockSpec((t, dr), rev(1)), pl.BlockSpec((t, dr), rev(0)), pl.BlockSpec((8, dr), prev),
                  full(conv_w), full(conv_b), full(wa_b), full(ba), full(wx_b), full(bx), full(la)],
        out_specs=[pl.BlockSpec((t, 2 * dr), rev(0)), acc((32, dr)), acc((8, dr)), acc((nh, hd, hd)), acc((nh, hd, hd)),
                   acc((8, dr)), acc((8, dr)), acc((8, dr))],
        out_shape=[_sds((s, din), BF16), _sds((32, dr), F32), _sds((8, dr), F32), _sds((nh, hd, hd), F32),
                   _sds((nh, hd, hd), F32), _sds((8, dr), F32), _sds((8, dr), F32), _sds((8, dr), F32)],
        scratch_shapes=[pltpu.VMEM((t + 8, dr), F32), pltpu.VMEM((t + 8, dr), F32), pltpu.VMEM((t, dr), F32),
                        pltpu.VMEM((t, dr), F32), pltpu.VMEM((hpr, t * 8, hd), F32), pltpu.VMEM((hpr, t * 8, hd), F32),
                        pltpu.VMEM((t + 8, dr), F32), pltpu.VMEM((hpr, 8, hd), F32)],
        compiler_params=_cparams(1), name="rglru_bwd")(dhg, z, z, z, hsave, hsave, conv_w, conv_b, wa_b, ba, wx_b, bx, la)


LANES = 128


def _scan_put(ref, j, t, val):
    for q in range(ref.shape[0]):
        ref[q, pl.ds(j, t, stride=8), :] = val[:, q * LANES:(q + 1) * LANES]


def _scan_get(ref, j, t):
    return jnp.concatenate([ref[q, pl.ds(j, t, stride=8), :] for q in range(ref.shape[0])], axis=1)


def _s5_fwd(z, bre, bim, cre, cim, lbr, lbi, dvec, s, ds_, t, ucol):
    uw = ds_ // SCAN_ROWS
    nq = lbr.shape[0]
    tile = (nq, 8, LANES)

    def body(u_ref, bre_ref, bim_ref, cre_ref, cim_ref, lbr_ref, lbi_ref, d_ref, ys_ref, yg_ref, hre_ref, him_ref,
             car_re, car_im):
        i = pl.program_id(0)

        @pl.when(i == 0)
        def _():
            car_re[...] = jnp.zeros(tile, F32)
            car_im[...] = jnp.zeros(tile, F32)

        for j in range(SCAN_ROWS):
            uj = u_ref[:, j * uw:(j + 1) * uw].astype(BF16)
            _scan_put(hre_ref, j, t, _dot(uj, bre_ref[j]))
            _scan_put(him_ref, j, t, _dot(uj, bim_ref[j]))
        lr = lbr_ref[...]
        li = lbi_ref[...]

        def step(tt, cr):
            hr, hi = cr
            o = pl.multiple_of(tt * 8, 8)
            nr = lr * hr - li * hi + hre_ref[:, pl.ds(o, 8), :]
            ni = lr * hi + li * hr + him_ref[:, pl.ds(o, 8), :]
            hre_ref[:, pl.ds(o, 8), :] = nr
            him_ref[:, pl.ds(o, 8), :] = ni
            return nr, ni

        cr, ci = lax.fori_loop(0, t, step, (car_re[...], car_im[...]), unroll=8)
        car_re[...] = cr
        car_im[...] = ci
        for j in range(SCAN_ROWS):
            js = slice(j * uw, (j + 1) * uw)
            hr = _scan_get(hre_ref, j, t).astype(BF16)
            hi = _scan_get(him_ref, j, t).astype(BF16)
            yv = _dot(hr, cre_ref[j]) - _dot(hi, cim_ref[j]) + d_ref[:, js] * u_ref[:, js]
            ys_ref[:, js] = yv
            yg_ref[:, js] = _gelu(yv).astype(BF16)

    full = lambda arr: pl.BlockSpec(arr.shape, lambda i: (0,) * arr.ndim)
    hblk = pl.BlockSpec((nq, t * 8, LANES), lambda i: (0, i, 0))
    return _pcall(
        body, grid=(s // t,),
        in_specs=[pl.BlockSpec((t, ds_), lambda i: (i, ucol)), full(bre), full(bim), full(cre), full(cim),
                  full(lbr), full(lbi), full(dvec)],
        out_specs=[pl.BlockSpec((t, ds_), lambda i: (i, 0)), pl.BlockSpec((t, ds_), lambda i: (i, 0)), hblk, hblk],
        out_shape=[_sds((s, ds_), F32), _sds((s, ds_), BF16), _sds((nq, s * 8, LANES), F32),
                   _sds((nq, s * 8, LANES), F32)],
        scratch_shapes=[pltpu.VMEM(tile, F32), pltpu.VMEM(tile, F32)],
        compiler_params=_cparams(1), name="s5_fwd")(z, bre, bim, cre, cim, lbr, lbi, dvec)


def _s5_bwd(dys, z, hre, him, bre, bim, cre, cim, lbr, lbi, dvec, dz, s, ds_, t, ucol):
    uw = ds_ // SCAN_ROWS
    nq = lbr.shape[0]
    tile = (nq, 8, LANES)
    nch = s // t

    def body(dy_ref, u_ref, hre_ref, him_ref, hpr_ref, hpi_ref, bre_ref, bim_ref, cre_ref, cim_ref, lbr_ref, lbi_ref,
             d_ref, dzin_ref, dz_ref, dbre_ref, dbim_ref, dcre_ref, dcim_ref, dlr_ref, dli_ref, dd_ref,
             gre, gim, car_re, car_im):
        i = pl.program_id(0)
        first = (nch - 1 - i) == 0

        @pl.when(i == 0)
        def _():
            for ref in (dbre_ref, dbim_ref, dcre_ref, dcim_ref, dlr_ref, dli_ref, dd_ref, car_re, car_im):
                ref[...] = jnp.zeros(ref.shape, F32)

        for j in range(SCAN_ROWS):
            dyj = dy_ref[:, j * uw:(j + 1) * uw].astype(BF16)
            _scan_put(gre, j, t, _dot_nt(dyj, cre_ref[j]))
            _scan_put(gim, j, t, -_dot_nt(dyj, cim_ref[j]))
        lr = lbr_ref[...]
        li = lbi_ref[...]

        def one(o, hm_re, hm_im, cr):
            c_re, c_im, a_lr, a_li = cr
            g_re = gre[:, pl.ds(o, 8), :] + c_re
            g_im = gim[:, pl.ds(o, 8), :] + c_im
            gre[:, pl.ds(o, 8), :] = g_re
            gim[:, pl.ds(o, 8), :] = g_im
            a_lr = a_lr + (g_re * hm_re + g_im * hm_im)
            a_li = a_li + (g_im * hm_re - g_re * hm_im)
            return lr * g_re + li * g_im, lr * g_im - li * g_re, a_lr, a_li

        def step(k, cr):
            o = pl.multiple_of((t - 1 - k) * 8, 8)
            om = pl.multiple_of((t - 2 - k) * 8, 8)
            return one(o, hre_ref[:, pl.ds(om, 8), :], him_ref[:, pl.ds(om, 8), :], cr)

        zero = jnp.zeros(tile, F32)
        cr = lax.fori_loop(0, t - 1, step, (car_re[...], car_im[...], zero, zero), unroll=8)
        keep = jnp.where(first, 0.0, 1.0)
        c_re, c_im, a_lr, a_li = one(0, hpr_ref[...] * keep, hpi_ref[...] * keep, cr)
        car_re[...] = c_re
        car_im[...] = c_im
        dlr_ref[...] += a_lr
        dli_ref[...] += a_li
        for j in range(SCAN_ROWS):
            js = slice(j * uw, (j + 1) * uw)
            g_r = _scan_get(gre, j, t).astype(BF16)
            g_i = _scan_get(gim, j, t).astype(BF16)
            dyj = dy_ref[:, js]
            uj = u_ref[:, js]
            du = _dot_nt(g_r, bre_ref[j]) + _dot_nt(g_i, bim_ref[j]) + d_ref[:, js] * dyj
            dz_ref[:, js] = du.astype(BF16)
            ujb, dyjb = uj.astype(BF16), dyj.astype(BF16)
            dbre_ref[j] += _dot_tn(ujb, g_r)
            dbim_ref[j] += _dot_tn(ujb, g_i)
            h_r = _scan_get(hre_ref, j, t).astype(BF16)
            h_i = _scan_get(him_ref, j, t).astype(BF16)
            dcre_ref[j] += _dot_tn(h_r, dyjb)
            dcim_ref[j] -= _dot_tn(h_i, dyjb)
            dd_ref[:, js] += _rows_part(dyj * uj)

    full = lambda arr: pl.BlockSpec(arr.shape, lambda i: (0,) * arr.ndim)
    acc = lambda shape: pl.BlockSpec(shape, lambda i: (0,) * len(shape))
    rev = lambda col: (lambda i: (nch - 1 - i, col))
    hblk = pl.BlockSpec((nq, t * 8, LANES), lambda i: (0, nch - 1 - i, 0))
    hprev = pl.BlockSpec(tile, lambda i: (0, jnp.maximum((nch - 1 - i) * t - 1, 0), 0))
    outs = _pcall(
        body, grid=(nch,),
        in_specs=[pl.BlockSpec((t, ds_), rev(0)), pl.BlockSpec((t, ds_), rev(ucol)), hblk, hblk, hprev, hprev,
                  full(bre), full(bim), full(cre), full(cim), full(lbr), full(lbi), full(dvec), ANY],
        out_specs=[pl.BlockSpec((t, ds_), rev(ucol)), acc(bre.shape), acc(bim.shape), acc(cre.shape), acc(cim.shape),
                   acc(tile), acc(tile), acc((8, ds_))],
        out_shape=[_sds(dz.shape, BF16), _sds(bre.shape, F32), _sds(bim.shape, F32), _sds(cre.shape, F32),
                   _sds(cim.shape, F32), _sds(tile, F32), _sds(tile, F32), _sds((8, ds_), F32)],
        scratch_shapes=[pltpu.VMEM((nq, t * 8, LANES), F32), pltpu.VMEM((nq, t * 8, LANES), F32),
                        pltpu.VMEM(tile, F32), pltpu.VMEM(tile, F32)],
        input_output_aliases={13: 0},
        compiler_params=_cparams(1), name="s5_bwd")(dys, z, hre, him, hre, him, bre, bim, cre, cim, lbr, lbi, dvec, dz)
    return outs


def _gate_a_bwd(dmix, ya, z, dz, s, d, tn, col0):
    tm = min(s, 512)

    def body(dm_ref, ya_ref, g_ref, dzin_ref, dz_ref):
        dm = dm_ref[...].astype(F32)
        sg = jax.nn.sigmoid(g_ref[...])
        dz_ref[...] = (dm * ya_ref[...] * sg * (1.0 - sg)).astype(BF16)

    blk = pl.BlockSpec((tm, tn), lambda i, j: (i, j))
    zblk = pl.BlockSpec((tm, tn), lambda i, j: (i, col0 + j))
    return _pcall(body, grid=(s // tm, d // tn), in_specs=[blk, blk, zblk, ANY], out_specs=zblk,
                  out_shape=_sds(dz.shape, BF16), input_output_aliases={3: 0},
                  compiler_params=_cparams(2), name="gate_a_bwd")(dmix, ya, z, dz)


def _gate_b_bwd(dmix, p, q, z, dz, s, d, tn, col0):
    tm = min(s, 512)

    def body(dm_ref, p_ref, q_ref, g_ref, dzin_ref, dz_ref, dp_ref, dq_ref):
        dm = dm_ref[...].astype(F32)
        sg = jax.nn.sigmoid(g_ref[...])
        sq = jax.nn.sigmoid(q_ref[...])
        pv = p_ref[...]
        dz_ref[...] = (dm * (pv * sq) * sg * (1.0 - sg)).astype(BF16)
        dyb = dm * sg
        dp_ref[...] = (dyb * sq).astype(BF16)
        dq_ref[...] = (dyb * pv * sq * (1.0 - sq)).astype(BF16)

    blk = pl.BlockSpec((tm, tn), lambda i, j: (i, j))
    zblk = pl.BlockSpec((tm, tn), lambda i, j: (i, col0 + j))
    return _pcall(body, grid=(s // tm, d // tn), in_specs=[blk, blk, blk, zblk, ANY], out_specs=[zblk, blk, blk],
                  out_shape=[_sds(dz.shape, BF16), _sds((s, d), BF16), _sds((s, d), BF16)],
                  input_output_aliases={4: 0}, compiler_params=_cparams(2), name="gate_b_bwd")(dmix, p, q, z, dz)


def _s5_disc(a_re, a_im, log_dt, b_re, b_im):
    dt = jnp.exp(log_dt)[:, None]
    lr = jnp.minimum(a_re, -1e-4)
    li = a_im
    mag = jnp.exp(lr * dt)
    lbr = mag * jnp.cos(li * dt)
    lbi = mag * jnp.sin(li * dt)
    zr, zi = lbr - 1.0, lbi
    den = lr * lr + li * li
    fr = (zr * lr + zi * li) / den
    fi = (zi * lr - zr * li) / den
    bbr = fr[..., None] * b_re - fi[..., None] * b_im
    bbi = fr[..., None] * b_im + fi[..., None] * b_re
    return lbr, lbi, bbr, bbi


def _s5_bmat(bb):
    g, p, h = bb.shape
    gpb = g // SCAN_ROWS
    eye = jnp.eye(gpb, dtype=F32)
    r = bb.reshape(SCAN_ROWS, gpb, p, h).transpose(0, 1, 3, 2)
    m = r[:, :, :, None, :] * eye[None, :, None, :, None]
    return m.reshape(SCAN_ROWS, gpb * h, gpb * p)


def _s5_bmat_diag(m, g, p, h):
    gpb = g // SCAN_ROWS
    eye = jnp.eye(gpb, dtype=F32)
    r = (m.reshape(SCAN_ROWS, gpb, h, gpb, p) * eye[None, :, None, :, None]).sum(axis=3)
    return r.transpose(0, 1, 3, 2).reshape(g, p, h)


def _s5_cmat(cc):
    g, h, p = cc.shape
    gpb = g // SCAN_ROWS
    eye = jnp.eye(gpb, dtype=F32)
    r = cc.reshape(SCAN_ROWS, gpb, h, p).transpose(0, 1, 3, 2)
    m = r[:, :, :, None, :] * eye[None, :, None, :, None]
    return m.reshape(SCAN_ROWS, gpb * p, gpb * h)


def _s5_cmat_diag(m, g, h, p):
    gpb = g // SCAN_ROWS
    eye = jnp.eye(gpb, dtype=F32)
    r = (m.reshape(SCAN_ROWS, gpb, p, gpb, h) * eye[None, :, None, :, None]).sum(axis=3)
    return r.transpose(0, 1, 3, 2).reshape(g, h, p)


PACK_W = 1024
PACK_TILE = 8 * PACK_W


def _pack(arrs, total_rows):
    parts = []
    for a in arrs:
        f = a.reshape(-1).astype(F32)
        pad = (-f.shape[0]) % PACK_TILE
        parts.append(jnp.pad(f, (0, pad)).reshape(-1, PACK_W))
    rows = sum(p.shape[0] for p in parts)
    if total_rows > rows:
        parts.append(jnp.zeros((total_rows - rows, PACK_W), F32))
    return jnp.concatenate(parts, axis=0)


def _unpack(buf, shapes):
    out, row = [], 0
    for shp in shapes:
        n = math.prod(shp)
        rows = -(-n // PACK_TILE) * 8
        out.append(buf[row:row + rows].reshape(-1)[:n].reshape(shp))
        row += rows
    return out


def _pack_rows(shapes):
    rows = sum(-(-math.prod(s) // PACK_TILE) * 8 for s in shapes)
    return -(-rows // 64) * 64


def kernel(x, w_in, conv_w, conv_b, rg_wa, rg_ba, rg_wx, rg_bx, rg_lambda, w_a_out, ssm_a_re, ssm_a_im, ssm_log_dt, ssm_b_re, ssm_b_im, ssm_c_re, ssm_c_im, ssm_d, glu_w, glu_v, w_out, ln1_g, ln1_b, mlp_w_up, mlp_b_up, mlp_w_down, mlp_b_down, ln2_g, ln2_b, loss_target, m_w_in, m_conv_w, m_conv_b, m_rg_wa, m_rg_ba, m_rg_wx, m_rg_bx, m_rg_lambda, m_w_a_out, m_ssm_a_re, m_ssm_a_im, m_ssm_log_dt, m_ssm_b_re, m_ssm_b_im, m_ssm_c_re, m_ssm_c_im, m_ssm_d, m_glu_w, m_glu_v, m_w_out, m_ln1_g, m_ln1_b, m_mlp_w_up, m_mlp_b_up, m_mlp_w_down, m_mlp_b_down, m_ln2_g, m_ln2_b, v_w_in, v_conv_w, v_conv_b, v_rg_wa, v_rg_ba, v_rg_wx, v_rg_bx, v_rg_lambda, v_w_a_out, v_ssm_a_re, v_ssm_a_im, v_ssm_log_dt, v_ssm_b_re, v_ssm_b_im, v_ssm_c_re, v_ssm_c_im, v_ssm_d, v_glu_w, v_glu_v, v_w_out, v_ln1_g, v_ln1_b, v_mlp_w_up, v_mlp_b_up, v_mlp_w_down, v_mlp_b_down, v_ln2_g, v_ln2_b):
    local = dict(locals())
    s, d = x.shape[1], x.shape[2]
    dr, ds_ = d, d // 2
    din4 = w_in.shape[2]
    din = N_CHIPS * din4
    df4 = mlp_w_up.shape[2]
    df = N_CHIPS * df4
    d4 = d // N_CHIPS
    nh, hd = rg_wa.shape[1], rg_wa.shape[2]
    grp, pst, gh = ssm_b_re.shape[1], ssm_b_re.shape[2], ssm_b_re.shape[3]
    tch = min(s, 256)
    tm = min(s, 1024)
    tmr = min(s, 256)
    xi, yi, ci = _me()
    c_arr = ci.reshape(1).astype(jnp.int32)
    chip_arr = (2 * xi + yi).reshape(1).astype(jnp.int32)
    sel_arr = jnp.stack([2 * xi + yi, ci]).astype(jnp.int32)

    x2 = x[0]
    tgt = loss_target[0]
    row = lambda a: a.reshape(1, -1)

    def gather(name, w):
        r, c = w.shape
        wb = _cast_bf16("cast_" + name, w, chip_arr)
        return _all_gather_chips("ag_" + name, wb.reshape(N_CHIPS, 2, r // 2, c)).reshape(N_CHIPS, r, c)

    w_in_s = gather("w_in", w_in[0])
    w_a_out_f = gather("w_a_out", w_a_out[0]).reshape(dr, d)
    glu_w_s = gather("glu_w", glu_w[0])
    glu_v_s = gather("glu_v", glu_v[0])
    w_out_f = gather("w_out", w_out[0]).reshape(d, d)
    w_up_s = gather("mlp_w_up", mlp_w_up[0])
    w_down_f = gather("mlp_w_down", mlp_w_down[0]).reshape(df, d)

    cw_place = jnp.zeros((4, dr), F32)
    cw_place = lax.dynamic_update_slice(cw_place, conv_w[0] * (ci == 0).astype(F32), (0, (2 * xi + yi) * d4))
    cw_rows = max(8, (4 * dr) // (8 * PACK_W))
    cw_pad = jnp.zeros((8 * cw_rows * PACK_W,), F32).at[:4 * dr].set(cw_place.reshape(-1))
    cw_full = _all_reduce_small("ar_conv_w", cw_pad.reshape(8, cw_rows, PACK_W)).reshape(-1)[:4 * dr].reshape(4, dr)

    lam = rg_lambda[0]
    la = row(jax.nn.softplus(-lam))
    wa_b = rg_wa[0].astype(BF16)
    wx_b = rg_wx[0].astype(BF16)
    disc_in = (ssm_a_re[0], ssm_a_im[0], ssm_log_dt[0], ssm_b_re[0], ssm_b_im[0])
    (lbr, lbi, bbr, bbi), disc_vjp = jax.vjp(_s5_disc, *disc_in)
    bre_m, bim_m = _s5_bmat(bbr).astype(BF16), _s5_bmat(bbi).astype(BF16)
    cre_m, cim_m = _s5_cmat(ssm_c_re[0]).astype(BF16), _s5_cmat(ssm_c_im[0]).astype(BF16)
    to_tile = lambda a: a.reshape(SCAN_ROWS, -1, LANES).transpose(1, 0, 2)
    from_tile = lambda a: a.transpose(1, 0, 2).reshape(grp, pst)
    lbr_t, lbi_t = to_tile(lbr), to_tile(lbi)
    dvec = row(ssm_d[0])

    tn_in = d // 8
    n4 = din4 // tn_in
    z, = _mm("in_proj", x2, w_in_s,
             pl.BlockSpec((tm, d), lambda i, j, k: (i, 0)),
             pl.BlockSpec((None, d, tn_in), lambda i, j, k: (j // n4, 0, j % n4)),
             NN, (s // tm, din // tn_in, 1), [_sds((s, din), F32)],
             [pl.BlockSpec((tm, tn_in), lambda i, j, k: (i, j))], _store())
    ucol = (2 * dr) // ds_
    hsave, hg = _rglru_fwd(z, cw_full, row(conv_b[0]), wa_b, row(rg_ba[0]), wx_b, row(rg_bx[0]), la, s, dr, tch)
    ys, yg, hre, him = _s5_fwd(z, bre_m, bim_m, cre_m, cim_m, lbr_t, lbi_t, dvec, s, ds_, tch, ucol)

    tn4 = d4
    ya, = _mm("ya", hg, w_a_out_f,
              pl.BlockSpec((tm, dr), lambda i, j, k: (i, 0)), pl.BlockSpec((dr, tn4), lambda i, j, k: (0, j)),
              NN, (s // tm, d // tn4, 1), [_sds((s, d), F32)], [pl.BlockSpec((tm, tn4), lambda i, j, k: (i, j))], _store())
    glu_spec = pl.BlockSpec((None, ds_, tn4), lambda i, j, k: (j, 0, 0))
    pp, = _mm("glu_p", yg, glu_w_s, pl.BlockSpec((tm, ds_), lambda i, j, k: (i, 0)), glu_spec,
              NN, (s // tm, N_CHIPS, 1), [_sds((s, d), F32)], [pl.BlockSpec((tm, tn4), lambda i, j, k: (i, j))], _store())
    col_ga = (2 * dr + ds_) // tn4
    col_gb = col_ga + d // tn4

    def mix_epi(acc, ex, outs):
        p_ref, ya_ref, ga_ref, gb_ref = ex
        outs[0][...] = acc
        mix = jax.nn.sigmoid(ga_ref[...]) * ya_ref[...] + jax.nn.sigmoid(gb_ref[...]) * (p_ref[...] * jax.nn.sigmoid(acc))
        outs[1][...] = mix.astype(BF16)

    blk4 = pl.BlockSpec((tm, tn4), lambda i, j, k: (i, j))
    qq, mixb = _mm("glu_q_mix", yg, glu_v_s, pl.BlockSpec((tm, ds_), lambda i, j, k: (i, 0)), glu_spec,
                   NN, (s // tm, N_CHIPS, 1), [_sds((s, d), F32), _sds((s, d), BF16)], [blk4, blk4], mix_epi,
                   extras=(pp, ya, z, z),
                   extra_specs=(blk4, blk4, pl.BlockSpec((tm, tn4), lambda i, j, k: (i, col_ga + j)),
                                pl.BlockSpec((tm, tn4), lambda i, j, k: (i, col_gb + j))))

    g1, b1, g2, b2 = row(ln1_g[0]), row(ln1_b[0]), row(ln2_g[0]), row(ln2_b[0])
    vec = lambda n: pl.BlockSpec((1, n), lambda i, j, k: (0, 0))
    rowblk = pl.BlockSpec((tmr, d), lambda i, j, k: (i, 0))
    colblk1 = pl.BlockSpec((tmr, 1), lambda i, j, k: (i, 0))
    part_blk = pl.BlockSpec((8, d), lambda i, j, k: (i, 0))

    def ln1_epi(acc, ex, outs):
        x_ref, g_ref, b_ref = ex
        r1 = ALPHA * x_ref[...] + acc
        mu = jnp.mean(r1, axis=-1, keepdims=True)
        cen = r1 - mu
        var = jnp.mean(cen * cen, axis=-1, keepdims=True)
        rstd = lax.rsqrt(var + LN_EPS)
        xh = cen * rstd
        outs[0][...] = xh
        outs[1][...] = (xh * g_ref[...] + b_ref[...]).astype(BF16)
        outs[2][...] = rstd

    xhat1, x1b, rstd1 = _mm("out_proj_ln1", mixb, w_out_f, rowblk, pl.BlockSpec((d, d), lambda i, j, k: (0, 0)),
                            NN, (s // tmr, 1, 1), [_sds((s, d), F32), _sds((s, d), BF16), _sds((s, 1), F32)],
                            [rowblk, rowblk, colblk1], ln1_epi, extras=(x2, g1, b1), extra_specs=(rowblk, vec(d), vec(d)))

    tnf = min(df4, 1024)
    nf4 = df4 // tnf

    def up_epi(acc, ex, outs):
        hp = acc + ex[0][...]
        rl = jnp.maximum(hp, 0.0)
        outs[0][...] = (rl * rl).astype(BF16)
        outs[1][...] = rl.astype(BF16)

    fblk = pl.BlockSpec((tm, tnf), lambda i, j, k: (i, j))
    hact, hrelu = _mm("mlp_up", x1b, w_up_s, pl.BlockSpec((tm, d), lambda i, j, k: (i, 0)),
                      pl.BlockSpec((None, d, tnf), lambda i, j, k: (j // nf4, 0, j % nf4)),
                      NN, (s // tm, df // tnf, 1), [_sds((s, df), BF16), _sds((s, df), BF16)], [fblk, fblk], up_epi,
                      extras=(row(mlp_b_up[0]),), extra_specs=(pl.BlockSpec((1, tnf), lambda i, j, k: (0, j)),))

    tkd = min(df, 1024)

    def down_epi(acc, ex, outs):
        xh1_ref, t_ref, g1_ref, b1_ref, g2_ref, b2_ref, bd_ref = ex
        x1 = xh1_ref[...] * g1_ref[...] + b1_ref[...]
        r2 = ALPHA * x1 + (acc + bd_ref[...])
        mu = jnp.mean(r2, axis=-1, keepdims=True)
        cen = r2 - mu
        var = jnp.mean(cen * cen, axis=-1, keepdims=True)
        rstd = lax.rsqrt(var + LN_EPS)
        xh2 = cen * rstd
        err = (xh2 * g2_ref[...] + b2_ref[...]) - t_ref[...]
        tot = 0.5 * jnp.sum(jnp.mean(err * err, axis=-1, keepdims=True))
        rows_i = lax.broadcasted_iota(jnp.int32, (8, 128), 0)
        cols_i = lax.broadcasted_iota(jnp.int32, (8, 128), 1)
        outs[5][...] = jnp.where((rows_i == 0) & (cols_i == 0), tot, 0.0)
        dy = err * (1.0 / d)
        outs[2][...] = _rows_part(dy * xh2)
        outs[3][...] = _rows_part(dy)
        dxh = dy * g2_ref[...]
        m1 = jnp.mean(dxh, axis=-1, keepdims=True)
        m2 = jnp.mean(dxh * xh2, axis=-1, keepdims=True)
        dr2 = rstd * (dxh - m1 - xh2 * m2)
        outs[0][...] = dr2
        outs[1][...] = dr2.astype(BF16)
        outs[4][...] = _rows_part(dr2)

    nrb = s // tmr
    dr2, dr2b, dg2p, db2p, dbdp, lossp = _mm(
        "mlp_down_ln2_loss", hact, w_down_f, pl.BlockSpec((tmr, tkd), lambda i, j, k: (i, k)),
        pl.BlockSpec((tkd, d), lambda i, j, k: (k, 0)), NN, (nrb, 1, df // tkd),
        [_sds((s, d), F32), _sds((s, d), BF16), _sds((nrb * 8, d), F32), _sds((nrb * 8, d), F32),
         _sds((nrb * 8, d), F32), _sds((nrb * 8, 128), F32)],
        [rowblk, rowblk, part_blk, part_blk, part_blk, pl.BlockSpec((8, 128), lambda i, j, k: (i, 0))], down_epi,
        extras=(xhat1, tgt, g1, b1, g2, b2, row(mlp_b_down[0])),
        extra_specs=(rowblk, rowblk, vec(d), vec(d), vec(d), vec(d), vec(d)), acc_shape=(tmr, d))

    def dh_epi(acc, ex, outs):
        dh = acc * (2.0 * ex[0][...].astype(F32))
        outs[0][...] = dh.astype(BF16)
        outs[1][...] = _rows_part(dh)

    ntb = s // tm
    dhpre, dbup_p = _mm("mlp_down_bwd", dr2b, w_down_f, pl.BlockSpec((tm, d), lambda i, j, k: (i, 0)),
                        pl.BlockSpec((tnf, d), lambda i, j, k: (j, 0)), NT, (ntb, df // tnf, 1),
                        [_sds((s, df), BF16), _sds((ntb * 8, df), F32)],
                        [fblk, pl.BlockSpec((8, tnf), lambda i, j, k: (i, j))], dh_epi,
                        extras=(hrelu,), extra_specs=(fblk,))

    tt = min(s, 1024)
    ntt = s // tt
    tkk = min(d, 1024)
    both = lambda shape: [_sds(shape, F32), _sds(shape, BF16)]
    sq_blk = pl.BlockSpec((tkk, tkk), lambda i, j, k: (i, j))
    g_down = _mm("grad_w_down", hact, dr2b, pl.BlockSpec((tt, tkk), lambda i, j, k: (k, i)),
                 pl.BlockSpec((tt, tkk), lambda i, j, k: (k, j)), TN, (df // tkk, d // tkk, ntt),
                 both((df, d)), [sq_blk, sq_blk], _store_both, acc_shape=(tkk, tkk))
    up_blk = pl.BlockSpec((None, tkk, tnf), lambda i, j, k: (j // nf4, i, j % nf4))
    g_up = _mm("grad_w_up", x1b, dhpre, pl.BlockSpec((tt, tkk), lambda i, j, k: (k, i)),
               pl.BlockSpec((tt, tnf), lambda i, j, k: (k, j)), TN, (d // tkk, df // tnf, ntt),
               both((N_CHIPS, d, df4)), [up_blk, up_blk], _store_both, acc_shape=(tkk, tnf))

    def ln1_bwd_epi(acc, ex, outs):
        dr2_ref, xh_ref, rs_ref, g_ref = ex
        dx1 = ALPHA * dr2_ref[...] + acc
        xh = xh_ref[...]
        outs[2][...] = _rows_part(dx1 * xh)
        outs[3][...] = _rows_part(dx1)
        dxh = dx1 * g_ref[...]
        m1 = jnp.mean(dxh, axis=-1, keepdims=True)
        m2 = jnp.mean(dxh * xh, axis=-1, keepdims=True)
        dr1 = rs_ref[...] * (dxh - m1 - xh * m2)
        outs[0][...] = dr1
        outs[1][...] = dr1.astype(BF16)

    dr1, dr1b, dg1p, db1p = _mm(
        "mlp_up_bwd_ln1_bwd", dhpre, w_up_s, pl.BlockSpec((tmr, tnf), lambda i, j, k: (i, k)),
        pl.BlockSpec((None, d, tnf), lambda i, j, k: (k // nf4, 0, k % nf4)), NT, (nrb, 1, df // tnf),
        [_sds((s, d), F32), _sds((s, d), BF16), _sds((nrb * 8, d), F32), _sds((nrb * 8, d), F32)],
        [rowblk, rowblk, part_blk, part_blk], ln1_bwd_epi,
        extras=(dr2, xhat1, rstd1, g1), extra_specs=(rowblk, rowblk, colblk1, vec(d)), acc_shape=(tmr, d))

    dmix, = _mm("out_proj_bwd", dr1b, w_out_f, pl.BlockSpec((tm, d), lambda i, j, k: (i, 0)),
                pl.BlockSpec((tn4, d), lambda i, j, k: (j, 0)), NT, (ntb, d // tn4, 1),
                [_sds((s, d), BF16)], [blk4], _store())
    g_out = _mm("grad_w_out", mixb, dr1b, pl.BlockSpec((tt, tkk), lambda i, j, k: (k, i)),
                pl.BlockSpec((tt, tkk), lambda i, j, k: (k, j)), TN, (d // tkk, d // tkk, ntt),
                both((d, d)), [sq_blk, sq_blk], _store_both, acc_shape=(tkk, tkk))

    def dya_body(dm_ref, g_ref, o_ref):
        o_ref[...] = (dm_ref[...].astype(F32) * jax.nn.sigmoid(g_ref[...])).astype(BF16)

    tme = min(s, 512)
    eblk = pl.BlockSpec((tme, tn4), lambda i, j: (i, j))
    dya = _pcall(dya_body, grid=(s // tme, d // tn4),
                 in_specs=[eblk, pl.BlockSpec((tme, tn4), lambda i, j: (i, col_ga + j))], out_specs=eblk,
                 out_shape=_sds((s, d), BF16), compiler_params=_cparams(2), name="dya")(dmix, z)
    g_a_out = _mm("grad_w_a_out", hg, dya, pl.BlockSpec((tt, tkk), lambda i, j, k: (k, i)),
                  pl.BlockSpec((tt, tkk), lambda i, j, k: (k, j)), TN, (dr // tkk, d // tkk, ntt),
                  both((dr, d)), [sq_blk, sq_blk], _store_both, acc_shape=(tkk, tkk))
    dhg, = _mm("a_out_bwd", dya, w_a_out_f, pl.BlockSpec((tm, d), lambda i, j, k: (i, 0)),
               pl.BlockSpec((tn4, d), lambda i, j, k: (j, 0)), NT, (ntb, dr // tn4, 1),
               [_sds((s, dr), F32)], [blk4], _store())
    dz, dcw_p, dcb_p, dwa, dwx, dba_p, dbx_p, dla_p = _rglru_bwd(
        dhg, z, hsave, cw_full, row(conv_b[0]), wa_b, row(rg_ba[0]), wx_b, row(rg_bx[0]), la, s, dr, din, tch)
    dz = _gate_a_bwd(dmix, ya, z, dz, s, d, tn4, col_ga)
    dz, dpb, dqb = _gate_b_bwd(dmix, pp, qq, z, dz, s, d, tn4, col_gb)

    glu_gblk = pl.BlockSpec((None, ds_, tn4), lambda i, j, k: (j, 0, 0))
    g_glu_w = _mm("grad_glu_w", yg, dpb, pl.BlockSpec((tt, ds_), lambda i, j, k: (k, 0)),
                  pl.BlockSpec((tt, tn4), lambda i, j, k: (k, j)), TN, (1, N_CHIPS, ntt),
                  both((N_CHIPS, ds_, d4)), [glu_gblk, glu_gblk], _store_both, acc_shape=(ds_, tn4))
    g_glu_v = _mm("grad_glu_v", yg, dqb, pl.BlockSpec((tt, ds_), lambda i, j, k: (k, 0)),
                  pl.BlockSpec((tt, tn4), lambda i, j, k: (k, j)), TN, (1, N_CHIPS, ntt),
                  both((N_CHIPS, ds_, d4)), [glu_gblk, glu_gblk], _store_both, acc_shape=(ds_, tn4))
    sblk = pl.BlockSpec((tm, ds_), lambda i, j, k: (i, 0))
    glu_bspec = pl.BlockSpec((None, ds_, tn4), lambda i, j, k: (k, 0, 0))
    dyg_p, = _mm("glu_w_bwd", dpb, glu_w_s, pl.BlockSpec((tm, tn4), lambda i, j, k: (i, k)), glu_bspec,
                 NT, (ntb, 1, N_CHIPS), [_sds((s, ds_), F32)], [sblk], _store(), acc_shape=(tm, ds_))

    def dys_epi(acc, ex, outs):
        outs[0][...] = (acc + ex[0][...]) * _gelu_grad(ex[1][...])

    dys, = _mm("glu_v_bwd_gelu_bwd", dqb, glu_v_s, pl.BlockSpec((tm, tn4), lambda i, j, k: (i, k)), glu_bspec,
               NT, (ntb, 1, N_CHIPS), [_sds((s, ds_), F32)], [sblk], dys_epi,
               extras=(dyg_p, ys), extra_specs=(sblk, sblk), acc_shape=(tm, ds_))
    dz, dbre, dbim, dcre, dcim, dlr, dli, ddp = _s5_bwd(dys, z, hre, him, bre_m, bim_m, cre_m, cim_m, lbr_t, lbi_t,
                                                      dvec, dz, s, ds_, tch, ucol)

    tni = din4 // 3
    ni4 = 3
    in_gblk = pl.BlockSpec((None, tkk, tni), lambda i, j, k: (j // ni4, i, j % ni4))
    g_in = _mm("grad_w_in", x2, dz, pl.BlockSpec((tt, tkk), lambda i, j, k: (k, i)),
               pl.BlockSpec((tt, tni), lambda i, j, k: (k, j)), TN, (d // tkk, din // tni, ntt),
               both((N_CHIPS, d, din4)), [in_gblk, in_gblk], _store_both, acc_shape=(tkk, tni))

    def dx_epi(acc, ex, outs):
        outs[0][...] = ALPHA * ex[0][...] + acc

    tmx = min(s, 512)
    xblk = pl.BlockSpec((tmx, d), lambda i, j, k: (i, 0))
    grad_x, = _mm("in_proj_bwd", dz, w_in_s, pl.BlockSpec((tmx, tni), lambda i, j, k: (i, k)),
                  pl.BlockSpec((None, d, tni), lambda i, j, k: (k // ni4, 0, k % ni4)), NT, (s // tmx, 1, din // tni),
                  [_sds((s, d), F32)], [xblk], dx_epi, extras=(dr1,), extra_specs=(xblk,), acc_shape=(tmx, d))

    stack = lambda g, r: g.reshape(N_CHIPS, 2, r // 2, g.shape[-1])
    big = [("w_in", g_in, d), ("w_a_out", g_a_out, dr // N_CHIPS), ("glu_w", g_glu_w, ds_), ("glu_v", g_glu_v, ds_),
           ("w_out", g_out, d4), ("mlp_w_up", g_up, d), ("mlp_w_down", g_down, df4)]
    names_big = [n for n, _, _ in big]
    g32 = [stack(g[0], r) for _, g, r in big]
    g16 = [stack(g[1], r) for _, g, r in big]
    recv1 = _rs_to_sibling("rs_sibling", g16)
    hsum = [_add_half("rs_add_half_" + n, g, r, c_arr) for n, g, r in zip(names_big, g32, recv1)]
    recv2 = _rs_to_chips("rs_chips", [h[1] for h in hsum])
    rsum = [_add_chips("rs_add_chips_" + n, h[0], r, sel_arr) for n, h, r in zip(names_big, hsum, recv2)]
    joined = _rs_join("rs_join", rsum)
    res = {}
    for n, gj in zip(names_big, joined):
        w = local[n][0]
        g = gj.reshape(w.shape)
        dl, nm, nv = _adamw("adamw_" + n, w, g, local["m_" + n][0], local["v_" + n][0])
        res[n] = (g[None], dl[None], nm[None], nv[None])

    s8 = lambda p: p.sum(axis=0)
    red_names = ["loss", "conv_w", "conv_b", "rg_wa", "rg_ba", "rg_wx", "rg_bx", "dla", "lbr", "lbi", "bbr", "bbi",
                 "c_re", "c_im", "ssm_d", "ln1_g", "ln1_b", "mlp_b_up", "mlp_b_down", "ln2_g", "ln2_b"]
    red_vals = [lossp.sum().reshape(1), dcw_p.reshape(4, 8, dr).sum(axis=1), s8(dcb_p), dwa, s8(dba_p), dwx, s8(dbx_p),
                s8(dla_p), from_tile(dlr), from_tile(dli), _s5_bmat_diag(dbre, grp, pst, gh),
                _s5_bmat_diag(dbim, grp, pst, gh), _s5_cmat_diag(dcre, grp, gh, pst), _s5_cmat_diag(dcim, grp, gh, pst),
                s8(ddp).reshape(grp, gh), s8(dg1p), s8(db1p), s8(dbup_p), s8(dbdp), s8(dg2p), s8(db2p)]
    red_shapes = [v.shape for v in red_vals]
    rows_r = _pack_rows(red_shapes)
    red = _all_reduce_small("ar_small", _pack(red_vals, rows_r).reshape(8, rows_r // 8, PACK_W))
    rv = dict(zip(red_names, _unpack(red.reshape(rows_r, PACK_W), red_shapes)))
    loss = rv["loss"][0]
    d_are, d_aim, d_ldt, d_bre, d_bim = disc_vjp((rv["lbr"], rv["lbi"], rv["bbr"], rv["bbi"]))
    small_g = {"conv_w": rv["conv_w"], "conv_b": rv["conv_b"], "rg_wa": rv["rg_wa"], "rg_ba": rv["rg_ba"].reshape(nh, hd),
               "rg_wx": rv["rg_wx"], "rg_bx": rv["rg_bx"].reshape(nh, hd), "rg_lambda": rv["dla"] * (-jax.nn.sigmoid(-lam)),
               "ssm_a_re": d_are, "ssm_a_im": d_aim, "ssm_log_dt": d_ldt, "ssm_b_re": d_bre, "ssm_b_im": d_bim,
               "ssm_c_re": rv["c_re"], "ssm_c_im": rv["c_im"], "ssm_d": rv["ssm_d"], "ln1_g": rv["ln1_g"],
               "ln1_b": rv["ln1_b"], "mlp_b_up": rv["mlp_b_up"], "mlp_b_down": rv["mlp_b_down"],
               "ln2_g": rv["ln2_g"], "ln2_b": rv["ln2_b"]}
    small_names = list(small_g)
    col0 = (2 * xi + yi) * d4

    def placed(n, a):
        if n != "conv_w":
            return a[0]
        return lax.dynamic_update_slice(jnp.zeros((4, dr), F32), a[0], (0, col0))

    sm_shapes = [small_g[n].shape for n in small_names]
    rows_s = _pack_rows(sm_shapes)
    packs = [_pack([small_g[n] for n in small_names], rows_s)]
    for pre in ("", "m_", "v_"):
        packs.append(_pack([placed(n, local[pre + n]) for n in small_names], rows_s))
    dl_p, nm_p, nv_p = _adamw("adamw_small", packs[1], packs[0], packs[2], packs[3])
    dl_s = dict(zip(small_names, _unpack(dl_p, sm_shapes)))
    nm_s = dict(zip(small_names, _unpack(nm_p, sm_shapes)))
    nv_s = dict(zip(small_names, _unpack(nv_p, sm_shapes)))
    for n in small_names:
        tup = (small_g[n], dl_s[n], nm_s[n], nv_s[n])
        if n == "conv_w":
            tup = tuple(lax.dynamic_slice(a, (0, col0), (4, d4)) for a in tup)
        res[n] = tuple(a.reshape(local[n].shape) for a in tup)

    order = ["w_in", "conv_w", "conv_b", "rg_wa", "rg_ba", "rg_wx", "rg_bx", "rg_lambda", "w_a_out", "ssm_a_re",
             "ssm_a_im", "ssm_log_dt", "ssm_b_re", "ssm_b_im", "ssm_c_re", "ssm_c_im", "ssm_d", "glu_w", "glu_v",
             "w_out", "ln1_g", "ln1_b", "mlp_w_up", "mlp_b_up", "mlp_w_down", "mlp_b_down", "ln2_g", "ln2_b"]
    outs = [loss, grad_x[None]]
    for part in range(4):
        outs += [res[n][part] for n in order]
    return tuple(outs)
```
